```python
import jax, jax.numpy as jnp
from jax import lax
import numpy as np

D_MODEL = 4096
BATCH = 4
SEQ = 2048
DEPTH = 1

HEAD_DIM = 128
N_HEADS = D_MODEL // HEAD_DIM
N_MOBA = N_HEADS // 2
N_FOX = N_HEADS - N_MOBA
D_MOBA = N_MOBA * HEAD_DIM
D_FOX = N_FOX * HEAD_DIM
D_IN = 3 * D_MOBA + 3 * D_FOX + N_FOX
MOBA_BLOCK = 256
MOBA_TOPK = 3
MOBA_QCHUNK = 16
FOX_QBLOCK = 128
PEER_HEADS = 8
PEER_NKEYS = 128
PEER_EXPERTS = PEER_NKEYS * PEER_NKEYS
PEER_TOPK = 16
PEER_DKEY = 256
PEER_HALF = PEER_DKEY // 2
PEER_TCHUNK = 128
N_MOD = 6
NORM_EPS = 1e-6
NEG_INF = -1e30

kernel_name = "hymba_moba_fox_peer_adaln"


def rms_norm(x, g):
    xf = x.astype(jnp.float32)
    y = xf * lax.rsqrt(jnp.mean(xf * xf, axis=-1, keepdims=True) + NORM_EPS)
    return (y * g.astype(jnp.float32)).astype(x.dtype)


def modulate(h, shift, scale):
    return h * (1 + scale[:, None, :]) + shift[:, None, :]


def split_heads(t, n):
    b, s, _ = t.shape
    return t.reshape(b, s, n, HEAD_DIM).transpose(0, 2, 1, 3)


def merge_heads(t):
    b, h, s, d = t.shape
    return t.transpose(0, 2, 1, 3).reshape(b, s, h * d)


def alibi_slopes(n):
    return 2.0 ** (-8.0 * jnp.arange(1, n + 1, dtype=jnp.float32) / n)


def moba_attention(q, k, v, slopes):
    B, H, S, hd = q.shape
    nb = -(-S // MOBA_BLOCK)
    sp = nb * MOBA_BLOCK
    pad = ((0, 0), (0, 0), (0, sp - S), (0, 0))
    kp = jnp.pad(k, pad)
    vp = jnp.pad(v, pad)
    counts = np.minimum(MOBA_BLOCK, S - np.arange(nb) * MOBA_BLOCK).astype(np.float32)
    kmean = kp.reshape(B, H, nb, MOBA_BLOCK, hd).astype(jnp.float32).sum(3) / counts[:, None]
    qblk = jnp.arange(S) // MOBA_BLOCK
    gate = jnp.einsum('bhsd,bhnd->bhsn', q.astype(jnp.float32), kmean)
    past = jnp.arange(nb)[None, :] < qblk[:, None]
    gate = jnp.where(past, gate, NEG_INF)
    ksel = min(MOBA_TOPK, nb)
    _, sel = lax.top_k(gate, ksel)
    sel_valid = sel < qblk[:, None]
    scale = HEAD_DIM ** -0.5
    offs = jnp.arange(MOBA_BLOCK)
    bidx = jnp.arange(B)[:, None, None]
    hidx = jnp.arange(H)[None, :, None]
    slope = slopes[:, None, None]

    def chunk(ci):
        t0 = ci * MOBA_QCHUNK
        qc = lax.dynamic_slice_in_dim(q, t0, MOBA_QCHUNK, axis=2)
        tpos = t0 + jnp.arange(MOBA_QCHUNK)
        b0 = (t0 // MOBA_BLOCK) * MOBA_BLOCK
        k_own = lax.dynamic_slice_in_dim(kp, b0, MOBA_BLOCK, axis=2)
        v_own = lax.dynamic_slice_in_dim(vp, b0, MOBA_BLOCK, axis=2)
        spos_own = b0 + offs
        dist_own = (tpos[:, None] - spos_own[None, :]).astype(jnp.float32)
        l_own = jnp.einsum('bhqd,bhkd->bhqk', qc, k_own).astype(jnp.float32) * scale - slope * dist_own
        l_own = jnp.where(spos_own[None, :] <= tpos[:, None], l_own, NEG_INF)
        sel_c = lax.dynamic_slice_in_dim(sel, t0, MOBA_QCHUNK, axis=2)
        val_c = lax.dynamic_slice_in_dim(sel_valid, t0, MOBA_QCHUNK, axis=2)
        spos_sel = (sel_c[..., None] * MOBA_BLOCK + offs).reshape(B, H, MOBA_QCHUNK, ksel * MOBA_BLOCK)
        flat = spos_sel.reshape(B, H, MOBA_QCHUNK * ksel * MOBA_BLOCK)
        k_sel = kp[bidx, hidx, flat].reshape(B, H, MOBA_QCHUNK, ksel * MOBA_BLOCK, hd)
        v_sel = vp[bidx, hidx, flat].reshape(B, H, MOBA_QCHUNK, ksel * MOBA_BLOCK, hd)
        dist_sel = (tpos[:, None] - spos_sel).astype(jnp.float32)
        l_sel = jnp.einsum('bhqd,bhqkd->bhqk', qc, k_sel).astype(jnp.float32) * scale - slope * dist_sel
        l_sel = jnp.where(jnp.repeat(val_c, MOBA_BLOCK, axis=-1), l_sel, NEG_INF)
        p = jax.nn.softmax(jnp.concatenate([l_own, l_sel], axis=-1), axis=-1).astype(v.dtype)
        return (jnp.einsum('bhqk,bhkd->bhqd', p[..., :MOBA_BLOCK], v_own)
                + jnp.einsum('bhqk,bhqkd->bhqd', p[..., MOBA_BLOCK:], v_sel))

    outs = lax.map(chunk, jnp.arange(S // MOBA_QCHUNK))
    return outs.transpose(1, 2, 0, 3, 4).reshape(B, H, S, hd)


def forgetting_attention(q, k, v, log_f):
    B, H, S, hd = q.shape
    F = jnp.cumsum(log_f, axis=-1)
    spos = jnp.arange(S)
    scale = HEAD_DIM ** -0.5

    def block(bi):
        t0 = bi * FOX_QBLOCK
        qb = lax.dynamic_slice_in_dim(q, t0, FOX_QBLOCK, axis=2)
        Fq = lax.dynamic_slice_in_dim(F, t0, FOX_QBLOCK, axis=2)
        tpos = t0 + jnp.arange(FOX_QBLOCK)
        l = jnp.einsum('bhqd,bhkd->bhqk', qb, k).astype(jnp.float32) * scale + Fq[..., None] - F[:, :, None, :]
        l = jnp.where(spos[None, :] <= tpos[:, None], l, NEG_INF)
        p = jax.nn.softmax(l, axis=-1).astype(v.dtype)
        return jnp.einsum('bhqk,bhkd->bhqd', p, v)

    outs = lax.map(block, jnp.arange(S // FOX_QBLOCK))
    return outs.transpose(1, 2, 0, 3, 4).reshape(B, H, S, hd)


def peer_ffn(h, w_pq, sub_keys, expert_u, expert_v):
    B, S, D = h.shape
    q = (h @ w_pq).reshape(B, S, PEER_HEADS, 2, PEER_HALF)
    s = jnp.einsum('bshcd,hcnd->bshcn', q, sub_keys).astype(jnp.float32)
    sv, si = lax.top_k(s, PEER_TOPK)
    cand = (sv[..., 0, :, None] + sv[..., 1, None, :]).reshape(B, S, PEER_HEADS, PEER_TOPK * PEER_TOPK)
    cidx = (si[..., 0, :, None] * PEER_NKEYS + si[..., 1, None, :]).reshape(B, S, PEER_HEADS, PEER_TOPK * PEER_TOPK)
    top_s, top_c = lax.top_k(cand, PEER_TOPK)
    eidx = jnp.take_along_axis(cidx, top_c, axis=-1)
    g = jax.nn.softmax(top_s, axis=-1)
    n = (B * S) // PEER_TCHUNK
    kk = PEER_HEADS * PEER_TOPK
    hf = h.reshape(n, PEER_TCHUNK, D)
    ef = eidx.reshape(n, PEER_TCHUNK, kk)
    gf = g.reshape(n, PEER_TCHUNK, kk)

    def chunk(args):
        hc, ec, gc = args
        u = expert_u[ec]
        a = jax.nn.gelu(jnp.einsum('td,tkd->tk', hc, u).astype(jnp.float32), approximate=False)
        w = (gc * a).astype(h.dtype)
        return jnp.einsum('tk,tkd->td', w, expert_v[ec])

    y = lax.map(chunk, (hf, ef, gf))
    return y.reshape(B, S, D)


def setup_inputs(seed: int = 0) -> dict:
    key = jax.random.key(seed)
    ks = jax.random.split(key, 18)
    f32 = jnp.float32
    D = D_MODEL
    nrm = lambda k, shp, s: jax.random.normal(k, shp, f32) * s
    return {
        "x": nrm(ks[0], (BATCH, SEQ, D), 1.0),
        "c": nrm(ks[1], (BATCH, D), 1.0),
        "w_ada": nrm(ks[2], (DEPTH, D, N_MOD * D), 0.5 * D ** -0.5),
        "b_ada": nrm(ks[3], (DEPTH, N_MOD * D), 0.01),
        "norm1_g": 1.0 + nrm(ks[4], (DEPTH, D), 0.02),
        "w_in": nrm(ks[5], (DEPTH, D, D_IN), D ** -0.5),
        "b_f": jax.random.uniform(ks[6], (DEPTH, N_FOX), f32, 1.0, 6.0),
        "q_norm_moba": 1.0 + nrm(ks[7], (DEPTH, HEAD_DIM), 0.02),
        "k_norm_moba": 1.0 + nrm(ks[8], (DEPTH, HEAD_DIM), 0.02),
        "q_norm_fox": 1.0 + nrm(ks[9], (DEPTH, HEAD_DIM), 0.02),
        "k_norm_fox": 1.0 + nrm(ks[10], (DEPTH, HEAD_DIM), 0.02),
        "w_out": nrm(ks[11], (DEPTH, D, D), D ** -0.5),
        "norm2_g": 1.0 + nrm(ks[12], (DEPTH, D), 0.02),
        "w_pq": nrm(ks[13], (DEPTH, D, PEER_HEADS * PEER_DKEY), D ** -0.5),
        "peer_sub_keys": nrm(ks[14], (DEPTH, PEER_HEADS, 2, PEER_NKEYS, PEER_HALF), PEER_HALF ** -0.5),
        "peer_u": nrm(ks[15], (DEPTH, PEER_EXPERTS, D), D ** -0.5),
        "peer_v": nrm(ks[16], (DEPTH, PEER_EXPERTS, D), PEER_HEADS ** -0.5),
    }


def reference(x, c, w_ada, b_ada, norm1_g, w_in, b_f, q_norm_moba, k_norm_moba,
              q_norm_fox, k_norm_fox, w_out, norm2_g, w_pq, peer_sub_keys, peer_u, peer_v):
    B, S, D = x.shape
    slopes = alibi_slopes(N_MOBA)
    c_act = jax.nn.silu(c)
    cuts = [D_MOBA, 2 * D_MOBA, 3 * D_MOBA, 3 * D_MOBA + D_FOX, 3 * D_MOBA + 2 * D_FOX, 3 * D_MOBA + 3 * D_FOX]
    for l in range(DEPTH):
        mod = (c_act @ w_ada[l] + b_ada[l]).reshape(B, N_MOD, D)
        sh1, sc1, g1 = mod[:, 0], mod[:, 1], mod[:, 2]
        sh2, sc2, g2 = mod[:, 3], mod[:, 4], mod[:, 5]

        h = modulate(rms_norm(x, norm1_g[l]), sh1, sc1)
        proj = h @ w_in[l]
        qa, ka, va, qf, kf, vf, fg = jnp.split(proj, cuts, axis=-1)
        qa = rms_norm(split_heads(qa, N_MOBA), q_norm_moba[l])
        ka = rms_norm(split_heads(ka, N_MOBA), k_norm_moba[l])
        va = split_heads(va, N_MOBA)
        qf = rms_norm(split_heads(qf, N_FOX), q_norm_fox[l])
        kf = rms_norm(split_heads(kf, N_FOX), k_norm_fox[l])
        vf = split_heads(vf, N_FOX)
        log_f = jax.nn.log_sigmoid((fg + b_f[l]).astype(jnp.float32)).transpose(0, 2, 1)
        o_moba = moba_attention(qa, ka, va, slopes)
        o_fox = forgetting_attention(qf, kf, vf, log_f)
        mix = jnp.concatenate([merge_heads(o_moba), merge_heads(o_fox)], axis=-1)
        x = x + g1[:, None, :] * (mix @ w_out[l])

        h2 = modulate(rms_norm(x, norm2_g[l]), sh2, sc2)
        x = x + g2[:, None, :] * peer_ffn(h2, w_pq[l], peer_sub_keys[l], peer_u[l], peer_v[l])
    return x
```

```python
import functools

import jax
import jax.numpy as jnp
import numpy as np
from jax import lax
from jax.experimental import pallas as pl
from jax.experimental.pallas import tpu as pltpu

F32 = jnp.float32
BF16 = jnp.bfloat16

HEAD_DIM = 128
MOBA_BLOCK = 256
MOBA_BLOCK_LOG2 = 8
MOBA_TOPK = 3
PEER_HEADS = 8
PEER_NKEYS = 128
PEER_TOPK = 16
PEER_HALF = 128
N_MOD = 6
NORM_EPS = 1e-6
NEG_INF = -1e30
MASK_BIG = 2.0 ** 100
ATTN_TILE = 256
FG_GROUP = 16

MIB = 1024 * 1024


def _cparams(n_axes, vmem_mib):
    return pltpu.CompilerParams(
        dimension_semantics=("arbitrary",) * n_axes,
        vmem_limit_bytes=int(vmem_mib * MIB),
    )


def _dot(a, b):
    return jnp.dot(a, b, preferred_element_type=F32)


def _dot_nt(a, b):
    return lax.dot_general(a, b, (((1,), (1,)), ((), ())), preferred_element_type=F32)


def _dot_tn(a, b):
    return lax.dot_general(a, b, (((0,), (0,)), ((), ())), preferred_element_type=F32)


def _split3(x):
    p1 = x.astype(BF16)
    r1 = x - p1.astype(F32)
    p2 = r1.astype(BF16)
    r2 = r1 - p2.astype(F32)
    p3 = r2.astype(BF16)
    return p1, p2, p3


def _rms(x, g):
    ms = jnp.mean(x * x, axis=-1, keepdims=True)
    return x * lax.rsqrt(ms + NORM_EPS) * g


def _adaln_kernel(c_ref, w_ref, b_ref, o_ref):
    c = c_ref[...]
    ca = c * (1.0 / (1.0 + jnp.exp(-c)))
    o_ref[...] = _dot(ca.astype(BF16), w_ref[...].astype(BF16)) + b_ref[...]


def _adaln(c_pad, w, b_row):
    rows, d = c_pad.shape
    n = w.shape[1]
    tn = 512
    return pl.pallas_call(
        _adaln_kernel,
        grid=(n // tn,),
        in_specs=[
            pl.BlockSpec((rows, d), lambda j: (0, 0)),
            pl.BlockSpec((d, tn), lambda j: (0, j)),
            pl.BlockSpec((1, tn), lambda j: (0, j)),
        ],
        out_specs=pl.BlockSpec((rows, tn), lambda j: (0, j)),
        out_shape=jax.ShapeDtypeStruct((rows, n), F32),
        compiler_params=_cparams(1, 40),
        name="adaln",
    )(c_pad, w, b_row)


def _norm_mod(x_ref, g_ref, mod_ref, row0):
    xf = x_ref[0]
    y = _rms(xf, g_ref[...])
    sh = mod_ref[0, row0:row0 + 1, :]
    sc = mod_ref[0, row0 + 1:row0 + 2, :]
    return y * (1.0 + sc) + sh


def _norm1_kernel(x_ref, g_ref, mod_ref, wfg_ref, bfg_ref, h_ref, fp_ref, carry_ref, *, inv_scale, n_fox):
    s_idx = pl.program_id(1)
    hb = _norm_mod(x_ref, g_ref, mod_ref, 0).astype(BF16)
    h_ref[0] = hb
    ts = hb.shape[0]

    @pl.when(s_idx == 0)
    def _():
        carry_ref[...] = jnp.zeros_like(carry_ref)

    z = _dot(hb, wfg_ref[...]) + bfg_ref[...]
    lf = jnp.minimum(z, 0.0) - jnp.log1p(jnp.exp(-jnp.abs(z)))
    r = lax.broadcasted_iota(jnp.int32, (ts, ts), 0)
    c = lax.broadcasted_iota(jnp.int32, (ts, ts), 1)
    tri = jnp.where(c <= r, 1.0, 0.0).astype(BF16)
    l1, l2, l3 = _split3(lf)
    f = (_dot(tri, l3) + _dot(tri, l2)) + _dot(tri, l1) + carry_ref[...]
    carry_ref[...] = f[ts - 1:ts, :]
    f1, f2, f3 = [p.astype(F32) for p in _split3(f * inv_scale)]
    lane = lax.broadcasted_iota(jnp.int32, f.shape, 1)
    packed = jnp.where(lane < FG_GROUP, f1,
              jnp.where(lane < 2 * FG_GROUP, f2,
               jnp.where(lane < 3 * FG_GROUP, f3,
                jnp.where(lane == 3 * FG_GROUP, 1.0, 0.0))))
    fp_ref[0] = packed.astype(BF16)


def _norm1(x, g_row, mod3, wfg3, bfg3, n_fox):
    b, s, d = x.shape
    ts = 512
    kern = functools.partial(_norm1_kernel, inv_scale=float(HEAD_DIM ** 0.5), n_fox=n_fox)
    return pl.pallas_call(
        kern,
        grid=(b, s // ts),
        in_specs=[
            pl.BlockSpec((1, ts, d), lambda i, j: (i, j, 0)),
            pl.BlockSpec((1, d), lambda i, j: (0, 0)),
            pl.BlockSpec((1, N_MOD, d), lambda i, j: (i, 0, 0)),
            pl.BlockSpec((d, 128), lambda i, j: (0, 0)),
            pl.BlockSpec((1, 128), lambda i, j: (0, 0)),
        ],
        out_specs=[
            pl.BlockSpec((1, ts, d), lambda i, j: (i, j, 0)),
            pl.BlockSpec((1, ts, 128), lambda i, j: (i, j, 0)),
        ],
        out_shape=[
            jax.ShapeDtypeStruct((b, s, d), BF16),
            jax.ShapeDtypeStruct((b, s, 128), BF16),
        ],
        scratch_shapes=[pltpu.VMEM((1, 128), F32)],
        compiler_params=_cparams(2, 48),
        name="norm1_fgate",
    )(x, g_row, mod3, wfg3, bfg3)


def _norm2_kernel(x_ref, g_ref, mod_ref, h_ref):
    h_ref[0] = _norm_mod(x_ref, g_ref, mod_ref, 3).astype(BF16)


def _norm2(x, g_row, mod3):
    b, s, d = x.shape
    ts = 512
    return pl.pallas_call(
        _norm2_kernel,
        grid=(b, s // ts),
        in_specs=[
            pl.BlockSpec((1, ts, d), lambda i, j: (i, j, 0)),
            pl.BlockSpec((1, d), lambda i, j: (0, 0)),
            pl.BlockSpec((1, N_MOD, d), lambda i, j: (i, 0, 0)),
        ],
        out_specs=pl.BlockSpec((1, ts, d), lambda i, j: (i, j, 0)),
        out_shape=jax.ShapeDtypeStruct((b, s, d), BF16),
        compiler_params=_cparams(2, 48),
        name="norm2",
    )(x, g_row, mod3)


def _mm_kernel(a_ref, b_ref, o_ref):
    o_ref[...] = _dot(a_ref[...], b_ref[...]).astype(o_ref.dtype)


def _mm_tiles(m, n):
    tm = 1024 if m % 1024 == 0 else m
    tn = 1024 if n % 1024 == 0 else (512 if n % 512 == 0 else n)
    return tm, tn


def _matmul(a, b, out_dtype, name):
    m, k = a.shape
    n = b.shape[1]
    tm, tn = _mm_tiles(m, n)
    return pl.pallas_call(
        _mm_kernel,
        grid=(m // tm, n // tn),
        in_specs=[
            pl.BlockSpec((tm, k), lambda i, j: (i, 0)),
            pl.BlockSpec((k, tn), lambda i, j: (0, j)),
        ],
        out_specs=pl.BlockSpec((tm, tn), lambda i, j: (i, j)),
        out_shape=jax.ShapeDtypeStruct((m, n), out_dtype),
        compiler_params=_cparams(2, 52),
        name=name,
    )(a, b)


def _outproj_kernel(oa_ref, ob_ref, wa_ref, wb_ref, x_ref, mod_ref, o_ref):
    acc = _dot(oa_ref[...], wa_ref[...]) + _dot(ob_ref[...], wb_ref[...])
    gate = mod_ref[0, 2:3, :]
    o_ref[...] = x_ref[...] + gate * acc


def _outproj(oa, ob, w, x2d, mod3, seq):
    m, ka = oa.shape
    kb = ob.shape[1]
    assert ka == kb
    n = w.shape[1]
    tm, tn = _mm_tiles(m, n)
    tm = min(tm, seq)
    tn = min(tn, 512)
    return pl.pallas_call(
        _outproj_kernel,
        grid=(m // tm, n // tn),
        in_specs=[
            pl.BlockSpec((tm, ka), lambda i, j: (i, 0)),
            pl.BlockSpec((tm, kb), lambda i, j: (i, 0)),
            pl.BlockSpec((ka, tn), lambda i, j: (0, j)),
            pl.BlockSpec((kb, tn), lambda i, j: (1, j)),
            pl.BlockSpec((tm, tn), lambda i, j: (i, j)),
            pl.BlockSpec((1, N_MOD, tn), lambda i, j: ((i * tm) // seq, 0, j)),
        ],
        out_specs=pl.BlockSpec((tm, tn), lambda i, j: (i, j)),
        out_shape=jax.ShapeDtypeStruct((m, n), F32),
        compiler_params=_cparams(2, 52),
        name="out_proj_residual",
    )(oa, ob, w, w, x2d, mod3)


def _attn_tiles(qa_ref, ka_ref, vt_ref, o_ref, scale, sub_fn):
    seq = qa_ref.shape[0]
    t = ATTN_TILE
    nq = seq // t
    row = lax.broadcasted_iota(jnp.int32, (t, t), 0)
    col = lax.broadcasted_iota(jnp.int32, (t, t), 1)
    causal = row <= col
    for qb in range(nq):
        qa = qa_ref[qb * t:(qb + 1) * t, :]
        m = l = acc = None
        for n in [qb] + list(range(qb)):
            ka = ka_ref[n * t:(n + 1) * t, :]
            st = _dot_nt(ka, qa) * scale
            if sub_fn is not None:
                st = st - sub_fn(qb - n)
            if n == qb:
                st = jnp.where(causal, st, NEG_INF)
            mt = jnp.max(st, axis=0, keepdims=True)
            vt = vt_ref[:, n * t:(n + 1) * t]
            if m is None:
                m = mt
                p = jnp.exp(st - m)
                l = jnp.sum(p, axis=0, keepdims=True)
                acc = _dot(vt, p.astype(BF16))
            else:
                m_new = jnp.maximum(m, mt)
                alpha = jnp.exp(m - m_new)
                p = jnp.exp(st - m_new)
                l = alpha * l + jnp.sum(p, axis=0, keepdims=True)
                acc = alpha * acc + _dot(vt, p.astype(BF16))
                m = m_new
        ot = acc * (1.0 / l)
        o_ref[qb * t:(qb + 1) * t, :] = ot.T.astype(o_ref.dtype)


def _moba_kernel(slopes_ref, q_ref, k_ref, v_ref, gq_ref, gk_ref, o_ref,
                 qa_ref, ka_ref, vt_ref, ad_ref):
    h = pl.program_id(1)
    seq = q_ref.shape[0]
    t = ATTN_TILE
    nb = seq // MOBA_BLOCK
    qn = _rms(q_ref[...], gq_ref[...])
    kn = _rms(k_ref[...], gk_ref[...])
    qb16 = qn.astype(BF16)
    kb16 = kn.astype(BF16)

    rid = lax.broadcasted_iota(jnp.int32, (16, HEAD_DIM), 0)
    kmean = jnp.zeros((16, HEAD_DIM), F32)
    for n in range(nb):
        blk = jnp.sum(kn[n * MOBA_BLOCK:(n + 1) * MOBA_BLOCK, :], axis=0, keepdims=True) / float(MOBA_BLOCK)
        kmean = jnp.where(rid == n, blk, kmean)
    gt = _dot_nt(kmean.astype(BF16), qb16)

    bidx = lax.broadcasted_iota(jnp.int32, (16, seq), 0)
    qblk = lax.shift_right_logical(lax.broadcasted_iota(jnp.int32, (16, seq), 1), MOBA_BLOCK_LOG2)
    rank = jnp.zeros((16, seq), F32)
    for n in range(nb):
        gn = gt[n:n + 1, :]
        beats = jnp.where(gn > gt, 1.0, jnp.where(gn == gt, jnp.where(bidx > n, 1.0, 0.0), 0.0))
        rank = rank + jnp.where(qblk > n, beats, 0.0)
    keep = jnp.where(bidx == qblk, 1.0,
                     jnp.where(bidx < qblk, jnp.where(rank < float(MOBA_TOPK), 1.0, 0.0), 0.0))
    bias_t = jnp.where(keep > 0.5, 0.0, -MASK_BIG)
    bias_pad = jnp.concatenate([bias_t, jnp.zeros((HEAD_DIM - 16, seq), F32)], axis=0)
    bias_nat = bias_pad.T

    qa_ref[:, 0:HEAD_DIM] = qb16
    qa_ref[:, HEAD_DIM:2 * HEAD_DIM] = bias_nat.astype(BF16)
    krow = lax.shift_right_logical(lax.broadcasted_iota(jnp.int32, (seq, HEAD_DIM), 0), MOBA_BLOCK_LOG2)
    klane = lax.broadcasted_iota(jnp.int32, (seq, HEAD_DIM), 1)
    ka_ref[:, 0:HEAD_DIM] = kb16
    ka_ref[:, HEAD_DIM:2 * HEAD_DIM] = jnp.where(klane == krow, 1.0, 0.0).astype(BF16)
    vt_ref[...] = v_ref[...].T.astype(BF16)

    slope = slopes_ref[h]
    dr = (lax.broadcasted_iota(jnp.int32, (t, t), 1) - lax.broadcasted_iota(jnp.int32, (t, t), 0)).astype(F32)
    for dlt in range(seq // t):
        ad_ref[dlt] = slope * (dr + float(dlt * t))

    _attn_tiles(qa_ref, ka_ref, vt_ref, o_ref, float(HEAD_DIM ** -0.5), lambda dlt: ad_ref[dlt])


def _fox_kernel(q_ref, k_ref, v_ref, gq_ref, gk_ref, fp_ref, o_ref, qa_ref, ka_ref, vt_ref):
    h = pl.program_id(1)
    qn = _rms(q_ref[...], gq_ref[...])
    kn = _rms(k_ref[...], gk_ref[...])
    packed = fp_ref[0]
    r = lax.broadcasted_iota(jnp.int32, (128, 128), 0)
    c = lax.broadcasted_iota(jnp.int32, (128, 128), 1)
    ones_row = r == 3 * FG_GROUP
    part = jnp.where(r == h, 0, jnp.where(r == FG_GROUP + h, 1, jnp.where(r == 2 * FG_GROUP + h, 2, -1)))
    pq = jnp.where(part == c, 1.0, jnp.where(ones_row & (c >= 3) & (c < 6), 1.0, 0.0)).astype(BF16)
    pk = jnp.where((part + 3 == c) & (part >= 0), -1.0, jnp.where(ones_row & (c < 3), 1.0, 0.0)).astype(BF16)
    qa_ref[:, 0:HEAD_DIM] = qn.astype(BF16)
    qa_ref[:, HEAD_DIM:2 * HEAD_DIM] = _dot(packed, pq).astype(BF16)
    ka_ref[:, 0:HEAD_DIM] = kn.astype(BF16)
    ka_ref[:, HEAD_DIM:2 * HEAD_DIM] = _dot(packed, pk).astype(BF16)
    vt_ref[...] = v_ref[...].T.astype(BF16)
    _attn_tiles(qa_ref, ka_ref, vt_ref, o_ref, float(HEAD_DIM ** -0.5), None)


def _attn_scratch(seq):
    return [
        pltpu.VMEM((seq, 2 * HEAD_DIM), BF16),
        pltpu.VMEM((seq, 2 * HEAD_DIM), BF16),
        pltpu.VMEM((HEAD_DIM, seq), BF16),
    ]


def _moba(proj, slopes, gq, gk, batch, seq, n_heads, col0):
    hspec = lambda off: pl.BlockSpec((seq, HEAD_DIM), lambda b, h: (b, col0 + off + h))
    return pl.pallas_call(
        _moba_kernel,
        grid=(batch, n_heads),
        in_specs=[
            pl.BlockSpec(memory_space=pltpu.SMEM),
            hspec(0), hspec(n_heads), hspec(2 * n_heads),
            pl.BlockSpec((1, HEAD_DIM), lambda b, h: (0, 0)),
            pl.BlockSpec((1, HEAD_DIM), lambda b, h: (0, 0)),
        ],
        out_specs=pl.BlockSpec((seq, HEAD_DIM), lambda b, h: (b, h)),
        out_shape=jax.ShapeDtypeStruct((batch * seq, n_heads * HEAD_DIM), BF16),
        scratch_shapes=_attn_scratch(seq) + [pltpu.VMEM((seq // ATTN_TILE, ATTN_TILE, ATTN_TILE), F32)],
        compiler_params=_cparams(2, 40),
        name="moba_attention",
    )(slopes, proj, proj, proj, gq, gk)


def _fox(proj, fpack, gq, gk, batch, seq, n_heads, col0):
    hspec = lambda off: pl.BlockSpec((seq, HEAD_DIM), lambda b, h: (b, col0 + off + h))
    return pl.pallas_call(
        _fox_kernel,
        grid=(batch, n_heads),
        in_specs=[
            hspec(0), hspec(n_heads), hspec(2 * n_heads),
            pl.BlockSpec((1, HEAD_DIM), lambda b, h: (0, 0)),
            pl.BlockSpec((1, HEAD_DIM), lambda b, h: (0, 0)),
            pl.BlockSpec((1, seq, 128), lambda b, h: (b, 0, 0)),
        ],
        out_specs=pl.BlockSpec((seq, HEAD_DIM), lambda b, h: (b, h)),
        out_shape=jax.ShapeDtypeStruct((batch * seq, n_heads * HEAD_DIM), BF16),
        scratch_shapes=_attn_scratch(seq),
        compiler_params=_cparams(2, 40),
        name="fox_attention",
    )(proj, proj, proj, gq, gk, fpack)


def _top16_sorted(st):
    tt = st.shape[1]
    rid = lax.broadcasted_iota(jnp.int32, (PEER_TOPK, tt), 0)
    sv = jnp.zeros((PEER_TOPK, tt), F32)
    work = st
    for r in range(PEER_TOPK):
        mx = jnp.max(work, axis=0, keepdims=True)
        sv = jnp.where(rid == r, mx, sv)
        if r + 1 < PEER_TOPK:
            work = jnp.where(work == mx, -jnp.inf, work)
    return sv


def _peer_select_kernel(qp_ref, keys_ref, s2_ref, e2_ref, th_ref, c1_ref):
    tt = qp_ref.shape[0]
    b8 = lax.broadcasted_iota(jnp.int32, (8, tt), 0)
    inf = jnp.inf
    for h in range(PEER_HEADS):
        q1 = qp_ref[:, (2 * h) * PEER_HALF:(2 * h + 1) * PEER_HALF].astype(BF16)
        q2 = qp_ref[:, (2 * h + 1) * PEER_HALF:(2 * h + 2) * PEER_HALF].astype(BF16)
        s1 = _dot_nt(keys_ref[2 * h], q1)
        s2 = _dot_nt(keys_ref[2 * h + 1], q2)
        sv0 = _top16_sorted(s1)
        sv1 = _top16_sorted(s2)
        sv1a, sv1b = sv1[0:8, :], sv1[8:16, :]
        row = lambda a: sv0[a:a + 1, :]
        cands = [(row(0) + sv1a, sv1a), (row(0) + sv1b, sv1b), (row(1) + sv1a, sv1a)]
        for a, nbv in ((2, 5), (3, 4), (4, 3), (5, 2), (6, 2), (7, 2)):
            cands.append((jnp.where(b8 < nbv, row(a) + sv1a, -inf), sv1a))
        top1 = sv1[0:1, :]
        hsum = sv0[8:16, :] + top1
        work = [g for g, _ in cands] + [hsum]
        tau = None
        for r in range(PEER_TOPK):
            mx = functools.reduce(jnp.maximum, work)
            tau = jnp.max(mx, axis=0, keepdims=True)
            if r + 1 < PEER_TOPK:
                work = [jnp.where(g == tau, -inf, g) for g in work]
        top = row(0) + top1
        zsum = jnp.zeros((1, tt), F32)
        for g in [g for g, _ in cands] + [hsum]:
            zsum = zsum + jnp.sum(jnp.where(g >= tau, jnp.exp(g - top), 0.0), axis=0, keepdims=True)
        inv_z = 1.0 / zsum
        th_rows = []
        th0 = jnp.minimum(jnp.min(jnp.where(cands[0][0] >= tau, sv1a, inf), axis=0, keepdims=True),
                          jnp.min(jnp.where(cands[1][0] >= tau, sv1b, inf), axis=0, keepdims=True))
        th_rows.append(th0)
        for g, s in cands[2:]:
            th_rows.append(jnp.min(jnp.where(g >= tau, s, inf), axis=0, keepdims=True))
        th_hi = jnp.where(hsum >= tau, top1, inf)
        th = jnp.full(s1.shape, inf, F32)
        for a in range(PEER_TOPK):
            ta = th_rows[a] if a < 8 else th_hi[a - 8:a - 7, :]
            th = jnp.where(s1 == row(a), ta, th)
        s2_ref[h] = s2
        e2_ref[h] = jnp.exp(s2 - top1)
        th_ref[h] = th
        c1_ref[h] = jnp.exp(s1 - row(0)) * inv_z


def _peer_select(qp, keys16):
    t = qp.shape[0]
    tt = 256
    shp = jax.ShapeDtypeStruct((PEER_HEADS, PEER_NKEYS, t), F32)
    spec = pl.BlockSpec((PEER_HEADS, PEER_NKEYS, tt), lambda i: (0, 0, i))
    return pl.pallas_call(
        _peer_select_kernel,
        grid=(t // tt,),
        in_specs=[
            pl.BlockSpec((tt, qp.shape[1]), lambda i: (i, 0)),
            pl.BlockSpec(keys16.shape, lambda i: (0, 0, 0)),
        ],
        out_specs=[spec, spec, spec, spec],
        out_shape=[shp, shp, shp, shp],
        compiler_params=_cparams(1, 40),
        name="peer_select",
    )(qp, keys16)


def _peer_dense_kernel(h_ref, u_ref, v_ref, s2_ref, e2_ref, th_ref, c1_ref, y_ref, wt_ref):
    e = pl.program_id(1)

    @pl.when(e == 0)
    def _():
        y_ref[...] = jnp.zeros_like(y_ref)

    te = u_ref.shape[0]
    at = _dot_nt(u_ref[...], h_ref[...])
    for ii in range(te // PEER_NKEYS):
        acc = None
        for h in range(PEER_HEADS):
            th = th_ref[h, 0, ii:ii + 1, :]
            c1 = c1_ref[h, 0, ii:ii + 1, :]
            w = jnp.where(s2_ref[h] >= th, e2_ref[h] * c1, 0.0)
            acc = w if acc is None else acc + w
        a = at[ii * PEER_NKEYS:(ii + 1) * PEER_NKEYS, :]
        gelu = 0.5 * a * (1.0 + lax.erf(a * float(np.sqrt(0.5))))
        wt_ref[ii * PEER_NKEYS:(ii + 1) * PEER_NKEYS, :] = (acc * gelu).astype(BF16)
    y_ref[...] += _dot_tn(wt_ref[...], v_ref[...])


def _peer_dense(h2, u16, v16, s2t, e2t, tht, c1t):
    t, d = h2.shape
    n_exp = u16.shape[0]
    tm = 512 if t % 512 == 0 else t
    te = 512
    k = te // PEER_NKEYS
    tht4 = tht.reshape(PEER_HEADS, PEER_NKEYS // k, k, t)
    c1t4 = c1t.reshape(PEER_HEADS, PEER_NKEYS // k, k, t)
    full = pl.BlockSpec((PEER_HEADS, PEER_NKEYS, tm), lambda i, e: (0, 0, i))
    rows = pl.BlockSpec((PEER_HEADS, 1, k, tm), lambda i, e: (0, e, 0, i))
    return pl.pallas_call(
        _peer_dense_kernel,
        grid=(t // tm, n_exp // te),
        in_specs=[
            pl.BlockSpec((tm, d), lambda i, e: (i, 0)),
            pl.BlockSpec((te, d), lambda i, e: (e, 0)),
            pl.BlockSpec((te, d), lambda i, e: (e, 0)),
            full, full, rows, rows,
        ],
        out_specs=pl.BlockSpec((tm, d), lambda i, e: (i, 0)),
        out_shape=jax.ShapeDtypeStruct((t, d), F32),
        scratch_shapes=[pltpu.VMEM((te, tm), BF16)],
        compiler_params=_cparams(2, 58),
        name="peer_dense",
    )(h2, u16, v16, s2t, e2t, tht4, c1t4)


def _final_kernel(x_ref, y_ref, mod_ref, o_ref):
    o_ref[0] = x_ref[0] + mod_ref[0, 5:6, :] * y_ref[0]


def _final_residual(x1, y, mod3):
    b, s, d = x1.shape
    ts = 256
    blk = pl.BlockSpec((1, ts, d), lambda i, j: (i, j, 0))
    return pl.pallas_call(
        _final_kernel,
        grid=(b, s // ts),
        in_specs=[blk, blk, pl.BlockSpec((1, N_MOD, d), lambda i, j: (i, 0, 0))],
        out_specs=blk,
        out_shape=jax.ShapeDtypeStruct((b, s, d), F32),
        compiler_params=_cparams(2, 40),
        name="peer_residual",
    )(x1, y, mod3)


def _layer(x, c8, w_ada, b_ada, norm1_g, w_in, b_f, qn_m, kn_m, qn_f, kn_f, w_out, norm2_g,
           w_pq, sub_keys, peer_u, peer_v):
    b, s, d = x.shape
    t = b * s
    n_heads = d // HEAD_DIM
    n_moba = n_heads // 2
    n_fox = n_heads - n_moba
    d_moba = n_moba * HEAD_DIM
    d_fox = n_fox * HEAD_DIM
    d_qkv = 3 * d_moba + 3 * d_fox
    assert s % MOBA_BLOCK == 0 and s % 512 == 0 and n_fox <= FG_GROUP and s // MOBA_BLOCK <= 16

    mod = _adaln(c8, w_ada, b_ada.reshape(1, -1))
    mod3 = mod[:b].reshape(b, N_MOD, d)

    w_fg = w_in[:, d_qkv:]
    wfg3 = jnp.zeros((d, 128), F32)
    bfg3 = jnp.zeros((1, 128), F32)
    for g in range(3):
        wfg3 = wfg3.at[:, g * FG_GROUP:g * FG_GROUP + n_fox].set(w_fg)
        bfg3 = bfg3.at[0, g * FG_GROUP:g * FG_GROUP + n_fox].set(b_f)
    h1, fpack = _norm1(x, norm1_g.reshape(1, d), mod3, wfg3.astype(BF16), bfg3, n_fox)

    proj = _matmul(h1.reshape(t, d), w_in[:, :d_qkv].astype(BF16), F32, "in_proj")

    slopes = 2.0 ** (-8.0 * jnp.arange(1, n_moba + 1, dtype=F32) / n_moba)
    o_moba = _moba(proj, slopes, qn_m.reshape(1, -1), kn_m.reshape(1, -1), b, s, n_moba, 0)
    o_fox = _fox(proj, fpack, qn_f.reshape(1, -1), kn_f.reshape(1, -1), b, s, n_fox, 3 * n_moba)

    x1 = _outproj(o_moba, o_fox, w_out.astype(BF16), x.reshape(t, d), mod3, s).reshape(b, s, d)

    h2 = _norm2(x1, norm2_g.reshape(1, d), mod3).reshape(t, d)
    qp = _matmul(h2, w_pq.astype(BF16), F32, "peer_query")
    keys16 = sub_keys.reshape(2 * PEER_HEADS, PEER_NKEYS, PEER_HALF).astype(BF16)
    s2t, e2t, tht, c1t = _peer_select(qp, keys16)
    y = _peer_dense(h2, peer_u.astype(BF16), peer_v.astype(BF16), s2t, e2t, tht, c1t)
    return _final_residual(x1, y.reshape(b, s, d), mod3)


def kernel(x, c, w_ada, b_ada, norm1_g, w_in, b_f, q_norm_moba, k_norm_moba, q_norm_fox, k_norm_fox,
           w_out, norm2_g, w_pq, peer_sub_keys, peer_u, peer_v):
    b = x.shape[0]
    c8 = jnp.zeros((8, c.shape[1]), F32).at[:b].set(c)
    for l in range(w_ada.shape[0]):
        x = _layer(x, c8, w_ada[l], b_ada[l], norm1_g[l], w_in[l], b_f[l], q_norm_moba[l], k_norm_moba[l],
                   q_norm_fox[l], k_norm_fox[l], w_out[l], norm2_g[l], w_pq[l], peer_sub_keys[l],
                   peer_u[l], peer_v[l])
    return x
```

```python
import functools

import jax
import jax.numpy as jnp
import numpy as np
from jax import lax
from jax.experimental import pallas as pl
from jax.experimental.pallas import tpu as pltpu

F32 = jnp.float32
BF16 = jnp.bfloat16

HEAD_DIM = 128
MOBA_BLOCK = 256
MOBA_BLOCK_LOG2 = 8
MOBA_TOPK = 3
PEER_HEADS = 8
PEER_NKEYS = 128
PEER_TOPK = 16
PEER_HALF = 128
N_MOD = 6
NORM_EPS = 1e-6
NEG_INF = -1e30
MASK_BIG = 2.0 ** 100
ATTN_TILE = 512
LOG2E = 1.4426950408889634
FG_GROUP = 16

MIB = 1024 * 1024


def _cparams(n_axes, vmem_mib):
    return pltpu.CompilerParams(
        dimension_semantics=("arbitrary",) * n_axes,
        vmem_limit_bytes=int(vmem_mib * MIB),
    )


def _dot(a, b):
    return jnp.dot(a, b, preferred_element_type=F32)


def _dot_nt(a, b):
    return lax.dot_general(a, b, (((1,), (1,)), ((), ())), preferred_element_type=F32)


def _dot_tn(a, b):
    return lax.dot_general(a, b, (((0,), (0,)), ((), ())), preferred_element_type=F32)


def _split3(x):
    p1 = x.astype(BF16)
    r1 = x - p1.astype(F32)
    p2 = r1.astype(BF16)
    r2 = r1 - p2.astype(F32)
    p3 = r2.astype(BF16)
    return p1, p2, p3


def _rms(x, g):
    ms = jnp.mean(x * x, axis=-1, keepdims=True)
    return x * lax.rsqrt(ms + NORM_EPS) * g


def _adaln_kernel(c_ref, w_ref, b_ref, o_ref):
    c = c_ref[...]
    ca = c * (1.0 / (1.0 + jnp.exp(-c)))
    o_ref[...] = _dot(ca.astype(BF16), w_ref[...].astype(BF16)) + b_ref[...]


def _adaln(c_pad, w, b_row):
    rows, d = c_pad.shape
    n = w.shape[1]
    tn = 512
    return pl.pallas_call(
        _adaln_kernel,
        grid=(n // tn,),
        in_specs=[
            pl.BlockSpec((rows, d), lambda j: (0, 0)),
            pl.BlockSpec((d, tn), lambda j: (0, j)),
            pl.BlockSpec((1, tn), lambda j: (0, j)),
        ],
        out_specs=pl.BlockSpec((rows, tn), lambda j: (0, j)),
        out_shape=jax.ShapeDtypeStruct((rows, n), F32),
        compiler_params=_cparams(1, 40),
        name="adaln",
    )(c_pad, w, b_row)


def _norm_mod(x_ref, g_ref, mod_ref, row0):
    xf = x_ref[0]
    y = _rms(xf, g_ref[...])
    sh = mod_ref[0, row0:row0 + 1, :]
    sc = mod_ref[0, row0 + 1:row0 + 2, :]
    return y * (1.0 + sc) + sh


def _norm1_kernel(x_ref, g_ref, mod_ref, wfg_ref, bfg_ref, h_ref, fp_ref, carry_ref, *, inv_scale, n_fox):
    s_idx = pl.program_id(1)
    hb = _norm_mod(x_ref, g_ref, mod_ref, 0).astype(BF16)
    h_ref[0] = hb
    ts = hb.shape[0]

    @pl.when(s_idx == 0)
    def _():
        carry_ref[...] = jnp.zeros_like(carry_ref)

    z = _dot(hb, wfg_ref[...]) + bfg_ref[...]
    lf = jnp.minimum(z, 0.0) - jnp.log1p(jnp.exp(-jnp.abs(z)))
    r = lax.broadcasted_iota(jnp.int32, (ts, ts), 0)
    c = lax.broadcasted_iota(jnp.int32, (ts, ts), 1)
    tri = jnp.where(c <= r, 1.0, 0.0).astype(BF16)
    l1, l2, l3 = _split3(lf)
    f = (_dot(tri, l3) + _dot(tri, l2)) + _dot(tri, l1) + carry_ref[...]
    carry_ref[...] = f[ts - 1:ts, :]
    f1, f2, f3 = [p.astype(F32) for p in _split3(f * inv_scale)]
    lane = lax.broadcasted_iota(jnp.int32, f.shape, 1)
    packed = jnp.where(lane < FG_GROUP, f1,
              jnp.where(lane < 2 * FG_GROUP, f2,
               jnp.where(lane < 3 * FG_GROUP, f3,
                jnp.where(lane == 3 * FG_GROUP, 1.0, 0.0))))
    fp_ref[0] = packed.astype(BF16)


def _norm1(x, g_row, mod3, wfg3, bfg3, n_fox):
    b, s, d = x.shape
    ts = 512
    kern = functools.partial(_norm1_kernel, inv_scale=float(HEAD_DIM ** 0.5), n_fox=n_fox)
    return pl.pallas_call(
        kern,
        grid=(b, s // ts),
        in_specs=[
            pl.BlockSpec((1, ts, d), lambda i, j: (i, j, 0)),
            pl.BlockSpec((1, d), lambda i, j: (0, 0)),
            pl.BlockSpec((1, N_MOD, d), lambda i, j: (i, 0, 0)),
            pl.BlockSpec((d, 128), lambda i, j: (0, 0)),
            pl.BlockSpec((1, 128), lambda i, j: (0, 0)),
        ],
        out_specs=[
            pl.BlockSpec((1, ts, d), lambda i, j: (i, j, 0)),
            pl.BlockSpec((1, ts, 128), lambda i, j: (i, j, 0)),
        ],
        out_shape=[
            jax.ShapeDtypeStruct((b, s, d), BF16),
            jax.ShapeDtypeStruct((b, s, 128), BF16),
        ],
        scratch_shapes=[pltpu.VMEM((1, 128), F32)],
        compiler_params=_cparams(2, 48),
        name="norm1_fgate",
    )(x, g_row, mod3, wfg3, bfg3)


def _norm2_kernel(x_ref, g_ref, mod_ref, h_ref):
    h_ref[0] = _norm_mod(x_ref, g_ref, mod_ref, 3).astype(BF16)


def _norm2(x, g_row, mod3):
    b, s, d = x.shape
    ts = 512
    return pl.pallas_call(
        _norm2_kernel,
        grid=(b, s // ts),
        in_specs=[
            pl.BlockSpec((1, ts, d), lambda i, j: (i, j, 0)),
            pl.BlockSpec((1, d), lambda i, j: (0, 0)),
            pl.BlockSpec((1, N_MOD, d), lambda i, j: (i, 0, 0)),
        ],
        out_specs=pl.BlockSpec((1, ts, d), lambda i, j: (i, j, 0)),
        out_shape=jax.ShapeDtypeStruct((b, s, d), BF16),
        compiler_params=_cparams(2, 48),
        name="norm2",
    )(x, g_row, mod3)


MM_TM = 1024
MM_TN = 512


def _cast_weight(w_ref, wb_ref):
    @pl.when(pl.program_id(1) == 0)
    def _():
        wb_ref[...] = w_ref[...].astype(BF16)


def _mm_kernel(a_ref, w_ref, o_ref, wb_ref):
    _cast_weight(w_ref, wb_ref)
    o_ref[...] = _dot(a_ref[...], wb_ref[...]).astype(o_ref.dtype)


def _matmul(a, w, n, out_dtype, name):
    m, k = a.shape
    tm = MM_TM if m % MM_TM == 0 else m
    tn = MM_TN
    assert n % tn == 0
    return pl.pallas_call(
        _mm_kernel,
        grid=(n // tn, m // tm),
        in_specs=[
            pl.BlockSpec((tm, k), lambda j, i: (i, 0)),
            pl.BlockSpec((k, tn), lambda j, i: (0, j)),
        ],
        out_specs=pl.BlockSpec((tm, tn), lambda j, i: (i, j)),
        out_shape=jax.ShapeDtypeStruct((m, n), out_dtype),
        scratch_shapes=[pltpu.VMEM((k, tn), BF16)],
        compiler_params=_cparams(2, 48),
        name=name,
    )(a, w)


def _outproj_kernel(oa_ref, ob_ref, w_ref, x_ref, mod_ref, o_ref, wb_ref):
    _cast_weight(w_ref, wb_ref)
    ka = oa_ref.shape[1]
    acc = _dot(oa_ref[...], wb_ref[0:ka, :]) + _dot(ob_ref[...], wb_ref[ka:, :])
    gate = mod_ref[0, 2:3, :]
    o_ref[...] = x_ref[...] + gate * acc


def _outproj(oa, ob, w, x2d, mod3, seq):
    m, ka = oa.shape
    kb = ob.shape[1]
    k, n = w.shape
    assert ka + kb == k
    tm = min(MM_TM if m % MM_TM == 0 else m, seq)
    tn = MM_TN
    return pl.pallas_call(
        _outproj_kernel,
        grid=(n // tn, m // tm),
        in_specs=[
            pl.BlockSpec((tm, ka), lambda j, i: (i, 0)),
            pl.BlockSpec((tm, kb), lambda j, i: (i, 0)),
            pl.BlockSpec((k, tn), lambda j, i: (0, j)),
            pl.BlockSpec((tm, tn), lambda j, i: (i, j)),
            pl.BlockSpec((1, N_MOD, tn), lambda j, i: ((i * tm) // seq, 0, j)),
        ],
        out_specs=pl.BlockSpec((tm, tn), lambda j, i: (i, j)),
        out_shape=jax.ShapeDtypeStruct((m, n), F32),
        scratch_shapes=[pltpu.VMEM((k, tn), BF16)],
        compiler_params=_cparams(2, 52),
        name="out_proj_residual",
    )(oa, ob, w, x2d, mod3)


def _attn_rows(qa_ref, ka_ref, vt_ref, o_ref, tab_ref):
    seq = qa_ref.shape[0]
    t = ATTN_TILE
    c_log2 = float(HEAD_DIM ** -0.5 * LOG2E)
    row = lax.broadcasted_iota(jnp.int32, (t, t), 0)
    col = lax.broadcasted_iota(jnp.int32, (t, t), 1)
    causal = row <= col
    for qp in range(seq // t):
        kv = (qp + 1) * t
        qa = qa_ref[qp * t:(qp + 1) * t, :]
        st = _dot_nt(ka_ref[0:kv, :], qa) * c_log2
        if tab_ref is not None:
            off = tab_ref.shape[0] - kv
            st = st - tab_ref[off:off + kv, :]
        tail = jnp.where(causal, st[kv - t:kv, :], NEG_INF)
        m = jnp.max(tail, axis=0, keepdims=True)
        if qp > 0:
            head = st[0:kv - t, :]
            m = jnp.maximum(m, jnp.max(head, axis=0, keepdims=True))
            p_head = jnp.exp2(head - m)
        p_tail = jnp.exp2(tail - m)
        l = jnp.sum(p_tail, axis=0, keepdims=True)
        acc = _dot(vt_ref[:, kv - t:kv], p_tail.astype(BF16))
        if qp > 0:
            l = l + jnp.sum(p_head, axis=0, keepdims=True)
            acc = acc + _dot(vt_ref[:, 0:kv - t], p_head.astype(BF16))
        ot = acc * (1.0 / l)
        o_ref[qp * t:(qp + 1) * t, :] = ot.T.astype(o_ref.dtype)


def _moba_kernel(slopes_ref, q_ref, k_ref, v_ref, gq_ref, gk_ref, o_ref,
                 qa_ref, ka_ref, vt_ref, tab_ref):
    h = pl.program_id(1)
    seq = q_ref.shape[0]
    t = ATTN_TILE
    nb = seq // MOBA_BLOCK
    qn = _rms(q_ref[...], gq_ref[...])
    kn = _rms(k_ref[...], gk_ref[...])
    qb16 = qn.astype(BF16)
    kb16 = kn.astype(BF16)

    rid = lax.broadcasted_iota(jnp.int32, (16, HEAD_DIM), 0)
    kmean = jnp.zeros((16, HEAD_DIM), F32)
    for n in range(nb):
        blk = jnp.sum(kn[n * MOBA_BLOCK:(n + 1) * MOBA_BLOCK, :], axis=0, keepdims=True) / float(MOBA_BLOCK)
        kmean = jnp.where(rid == n, blk, kmean)
    gt = _dot_nt(kmean.astype(BF16), qb16)

    bidx = lax.broadcasted_iota(jnp.int32, (16, seq), 0)
    qblk = lax.shift_right_logical(lax.broadcasted_iota(jnp.int32, (16, seq), 1), MOBA_BLOCK_LOG2)
    rank = jnp.zeros((16, seq), F32)
    for n in range(nb):
        gn = gt[n:n + 1, :]
        beats = jnp.where(gn > gt, 1.0, jnp.where(gn == gt, jnp.where(bidx > n, 1.0, 0.0), 0.0))
        rank = rank + jnp.where(qblk > n, beats, 0.0)
    keep = jnp.where(bidx == qblk, 1.0,
                     jnp.where(bidx < qblk, jnp.where(rank < float(MOBA_TOPK), 1.0, 0.0), 0.0))
    bias_t = jnp.where(keep > 0.5, 0.0, -MASK_BIG)
    bias_pad = jnp.concatenate([bias_t, jnp.zeros((HEAD_DIM - 16, seq), F32)], axis=0)
    bias_nat = bias_pad.T

    qa_ref[:, 0:HEAD_DIM] = qb16
    qa_ref[:, HEAD_DIM:2 * HEAD_DIM] = bias_nat.astype(BF16)
    krow = lax.shift_right_logical(lax.broadcasted_iota(jnp.int32, (seq, HEAD_DIM), 0), MOBA_BLOCK_LOG2)
    klane = lax.broadcasted_iota(jnp.int32, (seq, HEAD_DIM), 1)
    ka_ref[:, 0:HEAD_DIM] = kb16
    ka_ref[:, HEAD_DIM:2 * HEAD_DIM] = jnp.where(klane == krow, 1.0, 0.0).astype(BF16)
    vt_ref[...] = v_ref[...].T.astype(BF16)

    slope2 = slopes_ref[h] * LOG2E
    jrow = lax.broadcasted_iota(jnp.int32, (seq, t), 0)
    qcol = lax.broadcasted_iota(jnp.int32, (seq, t), 1)
    tab_ref[...] = slope2 * (qcol - jrow + (seq - t)).astype(F32)

    _attn_rows(qa_ref, ka_ref, vt_ref, o_ref, tab_ref)


def _fox_kernel(q_ref, k_ref, v_ref, gq_ref, gk_ref, fp_ref, o_ref, qa_ref, ka_ref, vt_ref):
    h = pl.program_id(1)
    qn = _rms(q_ref[...], gq_ref[...])
    kn = _rms(k_ref[...], gk_ref[...])
    packed = fp_ref[0]
    r = lax.broadcasted_iota(jnp.int32, (128, 128), 0)
    c = lax.broadcasted_iota(jnp.int32, (128, 128), 1)
    ones_row = r == 3 * FG_GROUP
    part = jnp.where(r == h, 0, jnp.where(r == FG_GROUP + h, 1, jnp.where(r == 2 * FG_GROUP + h, 2, -1)))
    pq = jnp.where(part == c, 1.0, jnp.where(ones_row & (c >= 3) & (c < 6), 1.0, 0.0)).astype(BF16)
    pk = jnp.where((part + 3 == c) & (part >= 0), -1.0, jnp.where(ones_row & (c < 3), 1.0, 0.0)).astype(BF16)
    qa_ref[:, 0:HEAD_DIM] = qn.astype(BF16)
    qa_ref[:, HEAD_DIM:2 * HEAD_DIM] = _dot(packed, pq).astype(BF16)
    ka_ref[:, 0:HEAD_DIM] = kn.astype(BF16)
    ka_ref[:, HEAD_DIM:2 * HEAD_DIM] = _dot(packed, pk).astype(BF16)
    vt_ref[...] = v_ref[...].T.astype(BF16)
    _attn_rows(qa_ref, ka_ref, vt_ref, o_ref, None)


def _attn_scratch(seq):
    return [
        pltpu.VMEM((seq, 2 * HEAD_DIM), BF16),
        pltpu.VMEM((seq, 2 * HEAD_DIM), BF16),
        pltpu.VMEM((HEAD_DIM, seq), BF16),
    ]


def _moba(proj, slopes, gq, gk, batch, seq, n_heads, col0):
    hspec = lambda off: pl.BlockSpec((seq, HEAD_DIM), lambda b, h: (b, col0 + off + h))
    return pl.pallas_call(
        _moba_kernel,
        grid=(batch, n_heads),
        in_specs=[
            pl.BlockSpec(memory_space=pltpu.SMEM),
            hspec(0), hspec(n_heads), hspec(2 * n_heads),
            pl.BlockSpec((1, HEAD_DIM), lambda b, h: (0, 0)),
            pl.BlockSpec((1, HEAD_DIM), lambda b, h: (0, 0)),
        ],
        out_specs=pl.BlockSpec((seq, HEAD_DIM), lambda b, h: (b, h)),
        out_shape=jax.ShapeDtypeStruct((batch * seq, n_heads * HEAD_DIM), BF16),
        scratch_shapes=_attn_scratch(seq) + [pltpu.VMEM((seq, ATTN_TILE), F32)],
        compiler_params=_cparams(2, 40),
        name="moba_attention",
    )(slopes, proj, proj, proj, gq, gk)


def _fox(proj, fpack, gq, gk, batch, seq, n_heads, col0):
    hspec = lambda off: pl.BlockSpec((seq, HEAD_DIM), lambda b, h: (b, col0 + off + h))
    return pl.pallas_call(
        _fox_kernel,
        grid=(batch, n_heads),
        in_specs=[
            hspec(0), hspec(n_heads), hspec(2 * n_heads),
            pl.BlockSpec((1, HEAD_DIM), lambda b, h: (0, 0)),
            pl.BlockSpec((1, HEAD_DIM), lambda b, h: (0, 0)),
            pl.BlockSpec((1, seq, 128), lambda b, h: (b, 0, 0)),
        ],
        out_specs=pl.BlockSpec((seq, HEAD_DIM), lambda b, h: (b, h)),
        out_shape=jax.ShapeDtypeStruct((batch * seq, n_heads * HEAD_DIM), BF16),
        scratch_shapes=_attn_scratch(seq),
        compiler_params=_cparams(2, 40),
        name="fox_attention",
    )(proj, proj, proj, gq, gk, fpack)


def _top16_sorted(st):
    tt = st.shape[1]
    rid = lax.broadcasted_iota(jnp.int32, (PEER_TOPK, tt), 0)
    sv = jnp.zeros((PEER_TOPK, tt), F32)
    work = st
    for r in range(PEER_TOPK):
        mx = jnp.max(work, axis=0, keepdims=True)
        sv = jnp.where(rid == r, mx, sv)
        if r + 1 < PEER_TOPK:
            work = jnp.where(work == mx, -jnp.inf, work)
    return sv


def _peer_select_kernel(qp_ref, keys_ref, s2_ref, e2_ref, th_ref, c1_ref):
    tt = qp_ref.shape[0]
    b8 = lax.broadcasted_iota(jnp.int32, (8, tt), 0)
    inf = jnp.inf
    for h in range(PEER_HEADS):
        q1 = qp_ref[:, (2 * h) * PEER_HALF:(2 * h + 1) * PEER_HALF].astype(BF16)
        q2 = qp_ref[:, (2 * h + 1) * PEER_HALF:(2 * h + 2) * PEER_HALF].astype(BF16)
        s1 = _dot_nt(keys_ref[2 * h], q1)
        s2 = _dot_nt(keys_ref[2 * h + 1], q2)
        sv0 = _top16_sorted(s1)
        sv1 = _top16_sorted(s2)
        sv1a, sv1b = sv1[0:8, :], sv1[8:16, :]
        row = lambda a: sv0[a:a + 1, :]
        cands = [(row(0) + sv1a, sv1a), (row(0) + sv1b, sv1b), (row(1) + sv1a, sv1a)]
        for a, nbv in ((2, 5), (3, 4), (4, 3), (5, 2), (6, 2), (7, 2)):
            cands.append((jnp.where(b8 < nbv, row(a) + sv1a, -inf), sv1a))
        top1 = sv1[0:1, :]
        hsum = sv0[8:16, :] + top1
        work = [g for g, _ in cands] + [hsum]
        tau = None
        for r in range(PEER_TOPK):
            mx = functools.reduce(jnp.maximum, work)
            tau = jnp.max(mx, axis=0, keepdims=True)
            if r + 1 < PEER_TOPK:
                work = [jnp.where(g == tau, -inf, g) for g in work]
        top = row(0) + top1
        zsum = jnp.zeros((1, tt), F32)
        for g in [g for g, _ in cands] + [hsum]:
            zsum = zsum + jnp.sum(jnp.where(g >= tau, jnp.exp(g - top), 0.0), axis=0, keepdims=True)
        inv_z = 1.0 / zsum
        th_rows = []
        th0 = jnp.minimum(jnp.min(jnp.where(cands[0][0] >= tau, sv1a, inf), axis=0, keepdims=True),
                          jnp.min(jnp.where(cands[1][0] >= tau, sv1b, inf), axis=0, keepdims=True))
        th_rows.append(th0)
        for g, s in cands[2:]:
            th_rows.append(jnp.min(jnp.where(g >= tau, s, inf), axis=0, keepdims=True))
        th_hi = jnp.where(hsum >= tau, top1, inf)
        th = jnp.full(s1.shape, inf, F32)
        for a in range(PEER_TOPK):
            ta = th_rows[a] if a < 8 else th_hi[a - 8:a - 7, :]
            th = jnp.where(s1 == row(a), ta, th)
        s2_ref[h] = s2
        e2_ref[h] = jnp.exp(s2 - top1)
        th_ref[h] = th
        c1_ref[h] = jnp.exp(s1 - row(0)) * inv_z


def _peer_select(qp, keys16):
    t = qp.shape[0]
    tt = 256
    shp = jax.ShapeDtypeStruct((PEER_HEADS, PEER_NKEYS, t), F32)
    spec = pl.BlockSpec((PEER_HEADS, PEER_NKEYS, tt), lambda i: (0, 0, i))
    return pl.pallas_call(
        _peer_select_kernel,
        grid=(t // tt,),
        in_specs=[
            pl.BlockSpec((tt, qp.shape[1]), lambda i: (i, 0)),
            pl.BlockSpec(keys16.shape, lambda i: (0, 0, 0)),
        ],
        out_specs=[spec, spec, spec, spec],
        out_shape=[shp, shp, shp, shp],
        compiler_params=_cparams(1, 40),
        name="peer_select",
    )(qp, keys16)


def _peer_dense_kernel(h_ref, u_ref, v_ref, s2_ref, e2_ref, th_ref, c1_ref, y_ref, wt_ref):
    e = pl.program_id(1)

    @pl.when(e == 0)
    def _():
        y_ref[...] = jnp.zeros_like(y_ref)

    te = u_ref.shape[0]
    n_i = te // PEER_NKEYS

    def body(row0):
        at = _dot_nt(u_ref[...], h_ref[...])
        for ii in range(n_i):
            r = row0 + ii
            acc = None
            for h in range(PEER_HEADS):
                th = th_ref[h, r:r + 1, :]
                c1 = c1_ref[h, r:r + 1, :]
                w = jnp.where(s2_ref[h] >= th, e2_ref[h] * c1, 0.0)
                acc = w if acc is None else acc + w
            a = at[ii * PEER_NKEYS:(ii + 1) * PEER_NKEYS, :]
            gelu = 0.5 * a * (1.0 + lax.erf(a * float(np.sqrt(0.5))))
            wt_ref[ii * PEER_NKEYS:(ii + 1) * PEER_NKEYS, :] = (acc * gelu).astype(BF16)
        y_ref[...] += _dot_tn(wt_ref[...], v_ref[...])

    phases = 8 // n_i
    for ph in range(phases):
        pl.when(e % phases == ph)(functools.partial(body, ph * n_i))


def _peer_dense(h2, u16, v16, s2t, e2t, tht, c1t):
    t, d = h2.shape
    n_exp = u16.shape[0]
    tm = 512 if t % 512 == 0 else t
    te = 512
    k = te // PEER_NKEYS
    full = pl.BlockSpec((PEER_HEADS, PEER_NKEYS, tm), lambda i, e: (0, 0, i))
    rows = pl.BlockSpec((PEER_HEADS, 8, tm), lambda i, e: (0, (e * k) // 8, i))
    return pl.pallas_call(
        _peer_dense_kernel,
        grid=(t // tm, n_exp // te),
        in_specs=[
            pl.BlockSpec((tm, d), lambda i, e: (i, 0)),
            pl.BlockSpec((te, d), lambda i, e: (e, 0)),
            pl.BlockSpec((te, d), lambda i, e: (e, 0)),
            full, full, rows, rows,
        ],
        out_specs=pl.BlockSpec((tm, d), lambda i, e: (i, 0)),
        out_shape=jax.ShapeDtypeStruct((t, d), F32),
        scratch_shapes=[pltpu.VMEM((te, tm), BF16)],
        compiler_params=_cparams(2, 58),
        name="peer_dense",
    )(h2, u16, v16, s2t, e2t, tht, c1t)


def _final_kernel(x_ref, y_ref, mod_ref, o_ref):
    o_ref[0] = x_ref[0] + mod_ref[0, 5:6, :] * y_ref[0]


def _final_residual(x1, y, mod3):
    b, s, d = x1.shape
    ts = 256
    blk = pl.BlockSpec((1, ts, d), lambda i, j: (i, j, 0))
    return pl.pallas_call(
        _final_kernel,
        grid=(b, s // ts),
        in_specs=[blk, blk, pl.BlockSpec((1, N_MOD, d), lambda i, j: (i, 0, 0))],
        out_specs=blk,
        out_shape=jax.ShapeDtypeStruct((b, s, d), F32),
        compiler_params=_cparams(2, 40),
        name="peer_residual",
    )(x1, y, mod3)


def _layer(x, c8, w_ada, b_ada, norm1_g, w_in, b_f, qn_m, kn_m, qn_f, kn_f, w_out, norm2_g,
           w_pq, sub_keys, peer_u, peer_v):
    b, s, d = x.shape
    t = b * s
    n_heads = d // HEAD_DIM
    n_moba = n_heads // 2
    n_fox = n_heads - n_moba
    d_moba = n_moba * HEAD_DIM
    d_fox = n_fox * HEAD_DIM
    d_qkv = 3 * d_moba + 3 * d_fox
    assert s % ATTN_TILE == 0 and ATTN_TILE % MOBA_BLOCK == 0 and s % 512 == 0
    assert n_fox <= FG_GROUP and s // MOBA_BLOCK <= 16 and b <= 8

    mod = _adaln(c8, w_ada, b_ada.reshape(1, -1))
    mod3 = mod[:b].reshape(b, N_MOD, d)

    w_fg = w_in[:, d_qkv:]
    wfg3 = jnp.zeros((d, 128), F32)
    bfg3 = jnp.zeros((1, 128), F32)
    for g in range(3):
        wfg3 = wfg3.at[:, g * FG_GROUP:g * FG_GROUP + n_fox].set(w_fg)
        bfg3 = bfg3.at[0, g * FG_GROUP:g * FG_GROUP + n_fox].set(b_f)
    h1, fpack = _norm1(x, norm1_g.reshape(1, d), mod3, wfg3.astype(BF16), bfg3, n_fox)

    proj = _matmul(h1.reshape(t, d), w_in, d_qkv, F32, "in_proj")

    slopes = 2.0 ** (-8.0 * jnp.arange(1, n_moba + 1, dtype=F32) / n_moba)
    o_moba = _moba(proj, slopes, qn_m.reshape(1, -1), kn_m.reshape(1, -1), b, s, n_moba, 0)
    o_fox = _fox(proj, fpack, qn_f.reshape(1, -1), kn_f.reshape(1, -1), b, s, n_fox, 3 * n_moba)

    x1 = _outproj(o_moba, o_fox, w_out, x.reshape(t, d), mod3, s).reshape(b, s, d)

    h2 = _norm2(x1, norm2_g.reshape(1, d), mod3).reshape(t, d)
    qp = _matmul(h2, w_pq, w_pq.shape[1], F32, "peer_query")
    keys16 = sub_keys.reshape(2 * PEER_HEADS, PEER_NKEYS, PEER_HALF).astype(BF16)
    s2t, e2t, tht, c1t = _peer_select(qp, keys16)
    y = _peer_dense(h2, peer_u.astype(BF16), peer_v.astype(BF16), s2t, e2t, tht, c1t)
    return _final_residual(x1, y.reshape(b, s, d), mod3)


def kernel(x, c, w_ada, b_ada, norm1_g, w_in, b_f, q_norm_moba, k_norm_moba, q_norm_fox, k_norm_fox,
           w_out, norm2_g, w_pq, peer_sub_keys, peer_u, peer_v):
    b = x.shape[0]
    c8 = jnp.zeros((8, c.shape[1]), F32).at[:b].set(c)
    for l in range(w_ada.shape[0]):
        x = _layer(x, c8, w_ada[l], b_ada[l], norm1_g[l], w_in[l], b_f[l], q_norm_moba[l], k_norm_moba[l],
                   q_norm_fox[l], k_norm_fox[l], w_out[l], norm2_g[l], w_pq[l], peer_sub_keys[l],
                   peer_u[l], peer_v[l])
    return x
```

```python
import functools

import jax
import jax.numpy as jnp
import numpy as np
from jax import lax
from jax.experimental import pallas as pl
from jax.experimental.pallas import tpu as pltpu

F32 = jnp.float32
BF16 = jnp.bfloat16

HEAD_DIM = 128
MOBA_BLOCK = 256
MOBA_BLOCK_LOG2 = 8
MOBA_TOPK = 3
PEER_HEADS = 8
PEER_NKEYS = 128
PEER_TOPK = 16
PEER_HALF = 128
N_MOD = 6
NORM_EPS = 1e-6
NEG_INF = -1e30
MASK_BIG = 2.0 ** 100
ATTN_TILE = 512
LOG2E = 1.4426950408889634
FG_GROUP = 16

MIB = 1024 * 1024


def _cparams(n_axes, vmem_mib):
    return pltpu.CompilerParams(
        dimension_semantics=("arbitrary",) * n_axes,
        vmem_limit_bytes=int(vmem_mib * MIB),
    )


def _dot(a, b):
    return jnp.dot(a, b, preferred_element_type=F32)


def _dot_nt(a, b):
    return lax.dot_general(a, b, (((1,), (1,)), ((), ())), preferred_element_type=F32)


def _dot_tn(a, b):
    return lax.dot_general(a, b, (((0,), (0,)), ((), ())), preferred_element_type=F32)


def _split3(x):
    p1 = x.astype(BF16)
    r1 = x - p1.astype(F32)
    p2 = r1.astype(BF16)
    r2 = r1 - p2.astype(F32)
    p3 = r2.astype(BF16)
    return p1, p2, p3


def _rms(x, g):
    ms = jnp.mean(x * x, axis=-1, keepdims=True)
    return x * lax.rsqrt(ms + NORM_EPS) * g


def _adaln_kernel(c_ref, w_ref, b_ref, o_ref):
    c = c_ref[...]
    ca = c * (1.0 / (1.0 + jnp.exp(-c)))
    o_ref[...] = _dot(ca.astype(BF16), w_ref[...].astype(BF16)) + b_ref[...]


def _adaln(c_pad, w, b_row):
    rows, d = c_pad.shape
    n = w.shape[1]
    tn = 512
    return pl.pallas_call(
        _adaln_kernel,
        grid=(n // tn,),
        in_specs=[
            pl.BlockSpec((rows, d), lambda j: (0, 0)),
            pl.BlockSpec((d, tn), lambda j: (0, j)),
            pl.BlockSpec((1, tn), lambda j: (0, j)),
        ],
        out_specs=pl.BlockSpec((rows, tn), lambda j: (0, j)),
        out_shape=jax.ShapeDtypeStruct((rows, n), F32),
        compiler_params=_cparams(1, 40),
        name="adaln",
    )(c_pad, w, b_row)


def _norm_mod(x_ref, g_ref, mod_ref, row0):
    xf = x_ref[0]
    y = _rms(xf, g_ref[...])
    sh = mod_ref[0, row0:row0 + 1, :]
    sc = mod_ref[0, row0 + 1:row0 + 2, :]
    return y * (1.0 + sc) + sh


def _norm1_kernel(x_ref, g_ref, mod_ref, wfg_ref, bfg_ref, h_ref, fp_ref, carry_ref, *, inv_scale, n_fox):
    s_idx = pl.program_id(1)
    hb = _norm_mod(x_ref, g_ref, mod_ref, 0).astype(BF16)
    h_ref[0] = hb
    ts = hb.shape[0]

    @pl.when(s_idx == 0)
    def _():
        carry_ref[...] = jnp.zeros_like(carry_ref)

    z = _dot_nt(hb, wfg_ref[...]) + bfg_ref[...]
    lf = jnp.minimum(z, 0.0) - jnp.log1p(jnp.exp(-jnp.abs(z)))
    r = lax.broadcasted_iota(jnp.int32, (ts, ts), 0)
    c = lax.broadcasted_iota(jnp.int32, (ts, ts), 1)
    tri = jnp.where(c <= r, 1.0, 0.0).astype(BF16)
    l1, l2, l3 = _split3(lf)
    f = (_dot(tri, l3) + _dot(tri, l2)) + _dot(tri, l1) + carry_ref[...]
    carry_ref[...] = f[ts - 1:ts, :]
    f1, f2, f3 = [p.astype(F32) for p in _split3(f * inv_scale)]
    lane = lax.broadcasted_iota(jnp.int32, f.shape, 1)
    packed = jnp.where(lane < FG_GROUP, f1,
              jnp.where(lane < 2 * FG_GROUP, f2,
               jnp.where(lane < 3 * FG_GROUP, f3,
                jnp.where(lane == 3 * FG_GROUP, 1.0, 0.0))))
    fp_ref[0] = packed.astype(BF16)


def _norm1(x, g_row, mod3, wfg3, bfg3, n_fox):
    b, s, d = x.shape
    ts = 512
    kern = functools.partial(_norm1_kernel, inv_scale=float(HEAD_DIM ** 0.5), n_fox=n_fox)
    return pl.pallas_call(
        kern,
        grid=(b, s // ts),
        in_specs=[
            pl.BlockSpec((1, ts, d), lambda i, j: (i, j, 0)),
            pl.BlockSpec((1, d), lambda i, j: (0, 0)),
            pl.BlockSpec((1, N_MOD, d), lambda i, j: (i, 0, 0)),
            pl.BlockSpec((128, d), lambda i, j: (0, 0)),
            pl.BlockSpec((1, 128), lambda i, j: (0, 0)),
        ],
        out_specs=[
            pl.BlockSpec((1, ts, d), lambda i, j: (i, j, 0)),
            pl.BlockSpec((1, ts, 128), lambda i, j: (i, j, 0)),
        ],
        out_shape=[
            jax.ShapeDtypeStruct((b, s, d), BF16),
            jax.ShapeDtypeStruct((b, s, 128), BF16),
        ],
        scratch_shapes=[pltpu.VMEM((1, 128), F32)],
        compiler_params=_cparams(2, 48),
        name="norm1_fgate",
    )(x, g_row, mod3, wfg3, bfg3)


def _norm2_kernel(x_ref, g_ref, mod_ref, h_ref):
    h_ref[0] = _norm_mod(x_ref, g_ref, mod_ref, 3).astype(BF16)


def _norm2(x, g_row, mod3):
    b, s, d = x.shape
    ts = 512
    return pl.pallas_call(
        _norm2_kernel,
        grid=(b, s // ts),
        in_specs=[
            pl.BlockSpec((1, ts, d), lambda i, j: (i, j, 0)),
            pl.BlockSpec((1, d), lambda i, j: (0, 0)),
            pl.BlockSpec((1, N_MOD, d), lambda i, j: (i, 0, 0)),
        ],
        out_specs=pl.BlockSpec((1, ts, d), lambda i, j: (i, j, 0)),
        out_shape=jax.ShapeDtypeStruct((b, s, d), BF16),
        compiler_params=_cparams(2, 48),
        name="norm2",
    )(x, g_row, mod3)


MM_TM = 1024
MM_TN = 512


def _cast_weight(w_ref, wb_ref):
    @pl.when(pl.program_id(1) == 0)
    def _():
        wb_ref[...] = w_ref[...].astype(BF16)


def _mm_kernel(a_ref, w_ref, o_ref, wb_ref, *, w_rows_are_outputs):
    _cast_weight(w_ref, wb_ref)
    dot = _dot_nt if w_rows_are_outputs else _dot
    o_ref[...] = dot(a_ref[...], wb_ref[...]).astype(o_ref.dtype)


def _matmul(a, w, n, out_dtype, name, w_rows_are_outputs=False):
    m, k = a.shape
    tm = MM_TM if m % MM_TM == 0 else m
    tn = MM_TN
    assert n % tn == 0
    if w_rows_are_outputs:
        w_block, w_spec = (tn, k), pl.BlockSpec((tn, k), lambda j, i: (j, 0))
    else:
        w_block, w_spec = (k, tn), pl.BlockSpec((k, tn), lambda j, i: (0, j))
    return pl.pallas_call(
        functools.partial(_mm_kernel, w_rows_are_outputs=w_rows_are_outputs),
        grid=(n // tn, m // tm),
        in_specs=[pl.BlockSpec((tm, k), lambda j, i: (i, 0)), w_spec],
        out_specs=pl.BlockSpec((tm, tn), lambda j, i: (i, j)),
        out_shape=jax.ShapeDtypeStruct((m, n), out_dtype),
        scratch_shapes=[pltpu.VMEM(w_block, BF16)],
        compiler_params=_cparams(2, 48),
        name=name,
    )(a, w)


def _outproj_kernel(oa_ref, ob_ref, w_ref, x_ref, mod_ref, o_ref, wb_ref):
    _cast_weight(w_ref, wb_ref)
    ka = oa_ref.shape[1]
    acc = _dot(oa_ref[...], wb_ref[0:ka, :]) + _dot(ob_ref[...], wb_ref[ka:, :])
    gate = mod_ref[0, 2:3, :]
    o_ref[...] = x_ref[...] + gate * acc


def _outproj(oa, ob, w, x2d, mod3, seq):
    m, ka = oa.shape
    kb = ob.shape[1]
    k, n = w.shape
    assert ka + kb == k
    tm = min(MM_TM if m % MM_TM == 0 else m, seq)
    tn = MM_TN
    return pl.pallas_call(
        _outproj_kernel,
        grid=(n // tn, m // tm),
        in_specs=[
            pl.BlockSpec((tm, ka), lambda j, i: (i, 0)),
            pl.BlockSpec((tm, kb), lambda j, i: (i, 0)),
            pl.BlockSpec((k, tn), lambda j, i: (0, j)),
            pl.BlockSpec((tm, tn), lambda j, i: (i, j)),
            pl.BlockSpec((1, N_MOD, tn), lambda j, i: ((i * tm) // seq, 0, j)),
        ],
        out_specs=pl.BlockSpec((tm, tn), lambda j, i: (i, j)),
        out_shape=jax.ShapeDtypeStruct((m, n), F32),
        scratch_shapes=[pltpu.VMEM((k, tn), BF16)],
        compiler_params=_cparams(2, 52),
        name="out_proj_residual",
    )(oa, ob, w, x2d, mod3)


def _attn_rows(qa_ref, ka_ref, vt_ref, o_ref, tab_ref):
    seq = qa_ref.shape[0]
    t = ATTN_TILE
    c_log2 = float(HEAD_DIM ** -0.5 * LOG2E)
    row = lax.broadcasted_iota(jnp.int32, (t, t), 0)
    col = lax.broadcasted_iota(jnp.int32, (t, t), 1)
    causal = row <= col
    for qp in range(seq // t):
        kv = (qp + 1) * t
        qa = qa_ref[qp * t:(qp + 1) * t, :]
        st = _dot_nt(ka_ref[0:kv, :], qa) * c_log2
        if tab_ref is not None:
            off = tab_ref.shape[0] - kv
            st = st - tab_ref[off:off + kv, :]
        tail = jnp.where(causal, st[kv - t:kv, :], NEG_INF)
        m = jnp.max(tail, axis=0, keepdims=True)
        if qp > 0:
            head = st[0:kv - t, :]
            m = jnp.maximum(m, jnp.max(head, axis=0, keepdims=True))
            p_head = jnp.exp2(head - m)
        p_tail = jnp.exp2(tail - m)
        l = jnp.sum(p_tail, axis=0, keepdims=True)
        acc = _dot(vt_ref[:, kv - t:kv], p_tail.astype(BF16))
        if qp > 0:
            l = l + jnp.sum(p_head, axis=0, keepdims=True)
            acc = acc + _dot(vt_ref[:, 0:kv - t], p_head.astype(BF16))
        ot = acc * (1.0 / l)
        o_ref[qp * t:(qp + 1) * t, :] = ot.T.astype(o_ref.dtype)


def _moba_kernel(slopes_ref, q_ref, k_ref, v_ref, gq_ref, gk_ref, o_ref,
                 qa_ref, ka_ref, vt_ref, tab_ref):
    h = pl.program_id(1)
    seq = q_ref.shape[0]
    t = ATTN_TILE
    nb = seq // MOBA_BLOCK
    qn = _rms(q_ref[...], gq_ref[...])
    kn = _rms(k_ref[...], gk_ref[...])
    qb16 = qn.astype(BF16)
    kb16 = kn.astype(BF16)

    rid = lax.broadcasted_iota(jnp.int32, (16, HEAD_DIM), 0)
    kmean = jnp.zeros((16, HEAD_DIM), F32)
    for n in range(nb):
        blk = jnp.sum(kn[n * MOBA_BLOCK:(n + 1) * MOBA_BLOCK, :], axis=0, keepdims=True) / float(MOBA_BLOCK)
        kmean = jnp.where(rid == n, blk, kmean)
    gt = _dot_nt(kmean.astype(BF16), qb16)

    bidx = lax.broadcasted_iota(jnp.int32, (16, seq), 0)
    qblk = lax.shift_right_logical(lax.broadcasted_iota(jnp.int32, (16, seq), 1), MOBA_BLOCK_LOG2)
    rank = jnp.zeros((16, seq), F32)
    for n in range(nb):
        gn = gt[n:n + 1, :]
        beats = jnp.where(gn > gt, 1.0, jnp.where(gn == gt, jnp.where(bidx > n, 1.0, 0.0), 0.0))
        rank = rank + jnp.where(qblk > n, beats, 0.0)
    keep = jnp.where(bidx == qblk, 1.0,
                     jnp.where(bidx < qblk, jnp.where(rank < float(MOBA_TOPK), 1.0, 0.0), 0.0))
    bias_t = jnp.where(keep > 0.5, 0.0, -MASK_BIG)
    bias_pad = jnp.concatenate([bias_t, jnp.zeros((HEAD_DIM - 16, seq), F32)], axis=0)
    bias_nat = bias_pad.T

    qa_ref[:, 0:HEAD_DIM] = qb16
    qa_ref[:, HEAD_DIM:2 * HEAD_DIM] = bias_nat.astype(BF16)
    krow = lax.shift_right_logical(lax.broadcasted_iota(jnp.int32, (seq, HEAD_DIM), 0), MOBA_BLOCK_LOG2)
    klane = lax.broadcasted_iota(jnp.int32, (seq, HEAD_DIM), 1)
    ka_ref[:, 0:HEAD_DIM] = kb16
    ka_ref[:, HEAD_DIM:2 * HEAD_DIM] = jnp.where(klane == krow, 1.0, 0.0).astype(BF16)
    vt_ref[...] = v_ref[...].T.astype(BF16)

    slope2 = slopes_ref[h] * LOG2E
    jrow = lax.broadcasted_iota(jnp.int32, (seq, t), 0)
    qcol = lax.broadcasted_iota(jnp.int32, (seq, t), 1)
    tab_ref[...] = slope2 * (qcol - jrow + (seq - t)).astype(F32)

    _attn_rows(qa_ref, ka_ref, vt_ref, o_ref, tab_ref)


def _fox_kernel(q_ref, k_ref, v_ref, gq_ref, gk_ref, fp_ref, o_ref, qa_ref, ka_ref, vt_ref):
    h = pl.program_id(1)
    qn = _rms(q_ref[...], gq_ref[...])
    kn = _rms(k_ref[...], gk_ref[...])
    packed = fp_ref[0]
    r = lax.broadcasted_iota(jnp.int32, (128, 128), 0)
    c = lax.broadcasted_iota(jnp.int32, (128, 128), 1)
    ones_row = r == 3 * FG_GROUP
    part = jnp.where(r == h, 0, jnp.where(r == FG_GROUP + h, 1, jnp.where(r == 2 * FG_GROUP + h, 2, -1)))
    pq = jnp.where(part == c, 1.0, jnp.where(ones_row & (c >= 3) & (c < 6), 1.0, 0.0)).astype(BF16)
    pk = jnp.where((part + 3 == c) & (part >= 0), -1.0, jnp.where(ones_row & (c < 3), 1.0, 0.0)).astype(BF16)
    qa_ref[:, 0:HEAD_DIM] = qn.astype(BF16)
    qa_ref[:, HEAD_DIM:2 * HEAD_DIM] = _dot(packed, pq).astype(BF16)
    ka_ref[:, 0:HEAD_DIM] = kn.astype(BF16)
    ka_ref[:, HEAD_DIM:2 * HEAD_DIM] = _dot(packed, pk).astype(BF16)
    vt_ref[...] = v_ref[...].T.astype(BF16)
    _attn_rows(qa_ref, ka_ref, vt_ref, o_ref, None)


def _attn_scratch(seq):
    return [
        pltpu.VMEM((seq, 2 * HEAD_DIM), BF16),
        pltpu.VMEM((seq, 2 * HEAD_DIM), BF16),
        pltpu.VMEM((HEAD_DIM, seq), BF16),
    ]


def _moba(proj, slopes, gq, gk, batch, seq, n_heads, col0):
    hspec = lambda off: pl.BlockSpec((seq, HEAD_DIM), lambda b, h: (b, col0 + off + h))
    return pl.pallas_call(
        _moba_kernel,
        grid=(batch, n_heads),
        in_specs=[
            pl.BlockSpec(memory_space=pltpu.SMEM),
            hspec(0), hspec(n_heads), hspec(2 * n_heads),
            pl.BlockSpec((1, HEAD_DIM), lambda b, h: (0, 0)),
            pl.BlockSpec((1, HEAD_DIM), lambda b, h: (0, 0)),
        ],
        out_specs=pl.BlockSpec((seq, HEAD_DIM), lambda b, h: (b, h)),
        out_shape=jax.ShapeDtypeStruct((batch * seq, n_heads * HEAD_DIM), BF16),
        scratch_shapes=_attn_scratch(seq) + [pltpu.VMEM((seq, ATTN_TILE), F32)],
        compiler_params=_cparams(2, 40),
        name="moba_attention",
    )(slopes, proj, proj, proj, gq, gk)


def _fox(proj, fpack, gq, gk, batch, seq, n_heads, col0):
    hspec = lambda off: pl.BlockSpec((seq, HEAD_DIM), lambda b, h: (b, col0 + off + h))
    return pl.pallas_call(
        _fox_kernel,
        grid=(batch, n_heads),
        in_specs=[
            hspec(0), hspec(n_heads), hspec(2 * n_heads),
            pl.BlockSpec((1, HEAD_DIM), lambda b, h: (0, 0)),
            pl.BlockSpec((1, HEAD_DIM), lambda b, h: (0, 0)),
            pl.BlockSpec((1, seq, 128), lambda b, h: (b, 0, 0)),
        ],
        out_specs=pl.BlockSpec((seq, HEAD_DIM), lambda b, h: (b, h)),
        out_shape=jax.ShapeDtypeStruct((batch * seq, n_heads * HEAD_DIM), BF16),
        scratch_shapes=_attn_scratch(seq),
        compiler_params=_cparams(2, 40),
        name="fox_attention",
    )(proj, proj, proj, gq, gk, fpack)


def _top16_sorted(st):
    tt = st.shape[1]
    rid = lax.broadcasted_iota(jnp.int32, (PEER_TOPK, tt), 0)
    sv = jnp.zeros((PEER_TOPK, tt), F32)
    work = st
    for r in range(PEER_TOPK):
        mx = jnp.max(work, axis=0, keepdims=True)
        sv = jnp.where(rid == r, mx, sv)
        if r + 1 < PEER_TOPK:
            work = jnp.where(work == mx, -jnp.inf, work)
    return sv


def _peer_select_kernel(qp_ref, keys_ref, s2_ref, e2_ref, th_ref, c1_ref):
    tt = qp_ref.shape[0]
    b8 = lax.broadcasted_iota(jnp.int32, (8, tt), 0)
    inf = jnp.inf
    for h in range(PEER_HEADS):
        q1 = qp_ref[:, (2 * h) * PEER_HALF:(2 * h + 1) * PEER_HALF].astype(BF16)
        q2 = qp_ref[:, (2 * h + 1) * PEER_HALF:(2 * h + 2) * PEER_HALF].astype(BF16)
        s1 = _dot_nt(keys_ref[2 * h], q1)
        s2 = _dot_nt(keys_ref[2 * h + 1], q2)
        sv0 = _top16_sorted(s1)
        sv1 = _top16_sorted(s2)
        sv1a, sv1b = sv1[0:8, :], sv1[8:16, :]
        row = lambda a: sv0[a:a + 1, :]
        cands = [(row(0) + sv1a, sv1a), (row(0) + sv1b, sv1b), (row(1) + sv1a, sv1a)]
        for a, nbv in ((2, 5), (3, 4), (4, 3), (5, 2), (6, 2), (7, 2)):
            cands.append((jnp.where(b8 < nbv, row(a) + sv1a, -inf), sv1a))
        top1 = sv1[0:1, :]
        hsum = sv0[8:16, :] + top1
        work = [g for g, _ in cands] + [hsum]
        tau = None
        for r in range(PEER_TOPK):
            mx = functools.reduce(jnp.maximum, work)
            tau = jnp.max(mx, axis=0, keepdims=True)
            if r + 1 < PEER_TOPK:
                work = [jnp.where(g == tau, -inf, g) for g in work]
        top = row(0) + top1
        zsum = jnp.zeros((1, tt), F32)
        for g in [g for g, _ in cands] + [hsum]:
            zsum = zsum + jnp.sum(jnp.where(g >= tau, jnp.exp(g - top), 0.0), axis=0, keepdims=True)
        inv_z = 1.0 / zsum
        th_rows = []
        th0 = jnp.minimum(jnp.min(jnp.where(cands[0][0] >= tau, sv1a, inf), axis=0, keepdims=True),
                          jnp.min(jnp.where(cands[1][0] >= tau, sv1b, inf), axis=0, keepdims=True))
        th_rows.append(th0)
        for g, s in cands[2:]:
            th_rows.append(jnp.min(jnp.where(g >= tau, s, inf), axis=0, keepdims=True))
        th_hi = jnp.where(hsum >= tau, top1, inf)
        th = jnp.full(s1.shape, inf, F32)
        for a in range(PEER_TOPK):
            ta = th_rows[a] if a < 8 else th_hi[a - 8:a - 7, :]
            th = jnp.where(s1 == row(a), ta, th)
        s2_ref[h] = s2
        e2_ref[h] = jnp.exp(s2 - top1)
        th_ref[h] = th
        c1_ref[h] = jnp.exp(s1 - row(0)) * inv_z


def _peer_select(qp, keys16):
    t = qp.shape[0]
    tt = 256
    shp = jax.ShapeDtypeStruct((PEER_HEADS, PEER_NKEYS, t), F32)
    spec = pl.BlockSpec((PEER_HEADS, PEER_NKEYS, tt), lambda i: (0, 0, i))
    return pl.pallas_call(
        _peer_select_kernel,
        grid=(t // tt,),
        in_specs=[
            pl.BlockSpec((tt, qp.shape[1]), lambda i: (i, 0)),
            pl.BlockSpec(keys16.shape, lambda i: (0, 0, 0)),
        ],
        out_specs=[spec, spec, spec, spec],
        out_shape=[shp, shp, shp, shp],
        compiler_params=_cparams(1, 40),
        name="peer_select",
    )(qp, keys16)


def _peer_dense_kernel(h_ref, u_ref, v_ref, s2_ref, e2_ref, th_ref, c1_ref, y_ref, wt_ref):
    e = pl.program_id(1)

    @pl.when(e == 0)
    def _():
        y_ref[...] = jnp.zeros_like(y_ref)

    te = u_ref.shape[0]
    n_i = te // PEER_NKEYS

    def body(row0):
        at = _dot_nt(u_ref[...], h_ref[...])
        for ii in range(n_i):
            r = row0 + ii
            acc = None
            for h in range(PEER_HEADS):
                th = th_ref[h, r:r + 1, :]
                c1 = c1_ref[h, r:r + 1, :]
                w = jnp.where(s2_ref[h] >= th, e2_ref[h] * c1, 0.0)
                acc = w if acc is None else acc + w
            a = at[ii * PEER_NKEYS:(ii + 1) * PEER_NKEYS, :]
            gelu = 0.5 * a * (1.0 + lax.erf(a * float(np.sqrt(0.5))))
            wt_ref[ii * PEER_NKEYS:(ii + 1) * PEER_NKEYS, :] = (acc * gelu).astype(BF16)
        y_ref[...] += _dot_tn(wt_ref[...], v_ref[...])

    phases = 8 // n_i
    for ph in range(phases):
        pl.when(e % phases == ph)(functools.partial(body, ph * n_i))


def _peer_dense(h2, u16, v16, s2t, e2t, tht, c1t):
    t, d = h2.shape
    n_exp = u16.shape[0]
    tm = 512 if t % 512 == 0 else t
    te = 512
    k = te // PEER_NKEYS
    full = pl.BlockSpec((PEER_HEADS, PEER_NKEYS, tm), lambda i, e: (0, 0, i))
    rows = pl.BlockSpec((PEER_HEADS, 8, tm), lambda i, e: (0, (e * k) // 8, i))
    return pl.pallas_call(
        _peer_dense_kernel,
        grid=(t // tm, n_exp // te),
        in_specs=[
            pl.BlockSpec((tm, d), lambda i, e: (i, 0)),
            pl.BlockSpec((te, d), lambda i, e: (e, 0)),
            pl.BlockSpec((te, d), lambda i, e: (e, 0)),
            full, full, rows, rows,
        ],
        out_specs=pl.BlockSpec((tm, d), lambda i, e: (i, 0)),
        out_shape=jax.ShapeDtypeStruct((t, d), F32),
        scratch_shapes=[pltpu.VMEM((te, tm), BF16)],
        compiler_params=_cparams(2, 58),
        name="peer_dense",
    )(h2, u16, v16, s2t, e2t, tht, c1t)


def _final_kernel(x_ref, y_ref, mod_ref, o_ref):
    o_ref[0] = x_ref[0] + mod_ref[0, 5:6, :] * y_ref[0]


def _final_residual(x1, y, mod3):
    b, s, d = x1.shape
    ts = 256
    blk = pl.BlockSpec((1, ts, d), lambda i, j: (i, j, 0))
    return pl.pallas_call(
        _final_kernel,
        grid=(b, s // ts),
        in_specs=[blk, blk, pl.BlockSpec((1, N_MOD, d), lambda i, j: (i, 0, 0))],
        out_specs=blk,
        out_shape=jax.ShapeDtypeStruct((b, s, d), F32),
        compiler_params=_cparams(2, 40),
        name="peer_residual",
    )(x1, y, mod3)


def _layer(x, c8, w_ada, b_ada, norm1_g, w_in, b_f, qn_m, kn_m, qn_f, kn_f, w_out, norm2_g,
           w_pq, sub_keys, peer_u, peer_v):
    b, s, d = x.shape
    t = b * s
    n_heads = d // HEAD_DIM
    n_moba = n_heads // 2
    n_fox = n_heads - n_moba
    d_moba = n_moba * HEAD_DIM
    d_fox = n_fox * HEAD_DIM
    d_qkv = 3 * d_moba + 3 * d_fox
    assert s % ATTN_TILE == 0 and ATTN_TILE % MOBA_BLOCK == 0 and s % 512 == 0
    assert n_fox <= FG_GROUP and s // MOBA_BLOCK <= 16 and b <= 8

    mod = _adaln(c8, w_ada, b_ada.reshape(1, -1))
    mod3 = mod[:b].reshape(b, N_MOD, d)

    w_in_t = jnp.swapaxes(w_in, 0, 1)
    w_fg_t = w_in_t[d_qkv:]
    wfg3 = jnp.zeros((128, d), F32)
    bfg3 = jnp.zeros((1, 128), F32)
    for g in range(3):
        wfg3 = wfg3.at[g * FG_GROUP:g * FG_GROUP + n_fox, :].set(w_fg_t)
        bfg3 = bfg3.at[0, g * FG_GROUP:g * FG_GROUP + n_fox].set(b_f)
    h1, fpack = _norm1(x, norm1_g.reshape(1, d), mod3, wfg3.astype(BF16), bfg3, n_fox)

    proj = _matmul(h1.reshape(t, d), w_in_t, d_qkv, F32, "in_proj", w_rows_are_outputs=True)

    slopes = 2.0 ** (-8.0 * jnp.arange(1, n_moba + 1, dtype=F32) / n_moba)
    o_moba = _moba(proj, slopes, qn_m.reshape(1, -1), kn_m.reshape(1, -1), b, s, n_moba, 0)
    o_fox = _fox(proj, fpack, qn_f.reshape(1, -1), kn_f.reshape(1, -1), b, s, n_fox, 3 * n_moba)

    x1 = _outproj(o_moba, o_fox, w_out, x.reshape(t, d), mod3, s).reshape(b, s, d)

    h2 = _norm2(x1, norm2_g.reshape(1, d), mod3).reshape(t, d)
    qp = _matmul(h2, w_pq, w_pq.shape[1], F32, "peer_query")
    keys16 = sub_keys.reshape(2 * PEER_HEADS, PEER_NKEYS, PEER_HALF).astype(BF16)
    s2t, e2t, tht, c1t = _peer_select(qp, keys16)
    y = _peer_dense(h2, peer_u.astype(BF16), peer_v.astype(BF16), s2t, e2t, tht, c1t)
    return _final_residual(x1, y.reshape(b, s, d), mod3)


def kernel(x, c, w_ada, b_ada, norm1_g, w_in, b_f, q_norm_moba, k_norm_moba, q_norm_fox, k_norm_fox,
           w_out, norm2_g, w_pq, peer_sub_keys, peer_u, peer_v):
    b = x.shape[0]
    c8 = jnp.zeros((8, c.shape[1]), F32).at[:b].set(c)
    for l in range(w_ada.shape[0]):
        x = _layer(x, c8, w_ada[l], b_ada[l], norm1_g[l], w_in[l], b_f[l], q_norm_moba[l], k_norm_moba[l],
                   q_norm_fox[l], k_norm_fox[l], w_out[l], norm2_g[l], w_pq[l], peer_sub_keys[l],
                   peer_u[l], peer_v[l])
    return x
```

```python
import functools

import jax
import jax.numpy as jnp
import numpy as np
from jax import lax
from jax.experimental import pallas as pl
from jax.experimental.pallas import tpu as pltpu

F32 = jnp.float32
BF16 = jnp.bfloat16

HEAD_DIM = 128
MOBA_BLOCK = 256
MOBA_BLOCK_LOG2 = 8
MOBA_TOPK = 3
PEER_HEADS = 8
PEER_NKEYS = 128
PEER_TOPK = 16
PEER_HALF = 128
N_MOD = 6
NORM_EPS = 1e-6
NEG_INF = -1e30
MASK_BIG = 2.0 ** 100
ATTN_TILE = 512
LOG2E = 1.4426950408889634
FG_GROUP = 16

MIB = 1024 * 1024


def _cparams(n_axes, vmem_mib):
    return pltpu.CompilerParams(
        dimension_semantics=("arbitrary",) * n_axes,
        vmem_limit_bytes=int(vmem_mib * MIB),
    )


def _dot(a, b):
    return jnp.dot(a, b, preferred_element_type=F32)


def _dot_nt(a, b):
    return lax.dot_general(a, b, (((1,), (1,)), ((), ())), preferred_element_type=F32)


def _dot_tn(a, b):
    return lax.dot_general(a, b, (((0,), (0,)), ((), ())), preferred_element_type=F32)


def _split3(x):
    p1 = x.astype(BF16)
    r1 = x - p1.astype(F32)
    p2 = r1.astype(BF16)
    r2 = r1 - p2.astype(F32)
    p3 = r2.astype(BF16)
    return p1, p2, p3


def _rms(x, g):
    ms = jnp.mean(x * x, axis=-1, keepdims=True)
    return x * lax.rsqrt(ms + NORM_EPS) * g


def _adaln_kernel(c_ref, w_ref, b_ref, o_ref):
    c = c_ref[...]
    ca = c * (1.0 / (1.0 + jnp.exp(-c)))
    o_ref[...] = _dot(ca.astype(BF16), w_ref[...].astype(BF16)) + b_ref[...]


def _adaln(c_pad, w, b_row):
    rows, d = c_pad.shape
    n = w.shape[1]
    tn = 512
    return pl.pallas_call(
        _adaln_kernel,
        grid=(n // tn,),
        in_specs=[
            pl.BlockSpec((rows, d), lambda j: (0, 0)),
            pl.BlockSpec((d, tn), lambda j: (0, j)),
            pl.BlockSpec((1, tn), lambda j: (0, j)),
        ],
        out_specs=pl.BlockSpec((rows, tn), lambda j: (0, j)),
        out_shape=jax.ShapeDtypeStruct((rows, n), F32),
        compiler_params=_cparams(1, 40),
        name="adaln",
    )(c_pad, w, b_row)


def _norm_mod(x_ref, g_ref, mod_ref, row0):
    xf = x_ref[0]
    y = _rms(xf, g_ref[...])
    sh = mod_ref[0, row0:row0 + 1, :]
    sc = mod_ref[0, row0 + 1:row0 + 2, :]
    return y * (1.0 + sc) + sh


def _norm1_kernel(x_ref, g_ref, mod_ref, wfg_ref, bfg_ref, h_ref, fp_ref, carry_ref, *, inv_scale, n_fox):
    s_idx = pl.program_id(1)
    hb = _norm_mod(x_ref, g_ref, mod_ref, 0).astype(BF16)
    h_ref[0] = hb
    ts = hb.shape[0]

    @pl.when(s_idx == 0)
    def _():
        carry_ref[...] = jnp.zeros_like(carry_ref)

    z = _dot_nt(hb, wfg_ref[...]) + bfg_ref[...]
    lf = jnp.minimum(z, 0.0) - jnp.log1p(jnp.exp(-jnp.abs(z)))
    r = lax.broadcasted_iota(jnp.int32, (ts, ts), 0)
    c = lax.broadcasted_iota(jnp.int32, (ts, ts), 1)
    tri = jnp.where(c <= r, 1.0, 0.0).astype(BF16)
    l1, l2, l3 = _split3(lf)
    f = (_dot(tri, l3) + _dot(tri, l2)) + _dot(tri, l1) + carry_ref[...]
    carry_ref[...] = f[ts - 1:ts, :]
    f1, f2, f3 = [p.astype(F32) for p in _split3(f * inv_scale)]
    lane = lax.broadcasted_iota(jnp.int32, f.shape, 1)
    packed = jnp.where(lane < FG_GROUP, f1,
              jnp.where(lane < 2 * FG_GROUP, f2,
               jnp.where(lane < 3 * FG_GROUP, f3,
                jnp.where(lane == 3 * FG_GROUP, 1.0, 0.0))))
    fp_ref[0] = packed.astype(BF16)


def _norm1(x, g_row, mod3, wfg3, bfg3, n_fox):
    b, s, d = x.shape
    ts = 512
    kern = functools.partial(_norm1_kernel, inv_scale=float(HEAD_DIM ** 0.5), n_fox=n_fox)
    return pl.pallas_call(
        kern,
        grid=(b, s // ts),
        in_specs=[
            pl.BlockSpec((1, ts, d), lambda i, j: (i, j, 0)),
            pl.BlockSpec((1, d), lambda i, j: (0, 0)),
            pl.BlockSpec((1, N_MOD, d), lambda i, j: (i, 0, 0)),
            pl.BlockSpec((128, d), lambda i, j: (0, 0)),
            pl.BlockSpec((1, 128), lambda i, j: (0, 0)),
        ],
        out_specs=[
            pl.BlockSpec((1, ts, d), lambda i, j: (i, j, 0)),
            pl.BlockSpec((1, ts, 128), lambda i, j: (i, j, 0)),
        ],
        out_shape=[
            jax.ShapeDtypeStruct((b, s, d), BF16),
            jax.ShapeDtypeStruct((b, s, 128), BF16),
        ],
        scratch_shapes=[pltpu.VMEM((1, 128), F32)],
        compiler_params=_cparams(2, 48),
        name="norm1_fgate",
    )(x, g_row, mod3, wfg3, bfg3)


def _norm2_kernel(x_ref, g_ref, mod_ref, h_ref):
    h_ref[0] = _norm_mod(x_ref, g_ref, mod_ref, 3).astype(BF16)


def _norm2(x, g_row, mod3):
    b, s, d = x.shape
    ts = 512
    return pl.pallas_call(
        _norm2_kernel,
        grid=(b, s // ts),
        in_specs=[
            pl.BlockSpec((1, ts, d), lambda i, j: (i, j, 0)),
            pl.BlockSpec((1, d), lambda i, j: (0, 0)),
            pl.BlockSpec((1, N_MOD, d), lambda i, j: (i, 0, 0)),
        ],
        out_specs=pl.BlockSpec((1, ts, d), lambda i, j: (i, j, 0)),
        out_shape=jax.ShapeDtypeStruct((b, s, d), BF16),
        compiler_params=_cparams(2, 48),
        name="norm2",
    )(x, g_row, mod3)


MM_TM = 1024
MM_TN = 512


def _cast_weight(w_ref, wb_ref):
    @pl.when(pl.program_id(1) == 0)
    def _():
        wb_ref[...] = w_ref[...].astype(BF16)


def _mm_kernel(a_ref, w_ref, o_ref, wb_ref, *, w_rows_are_outputs):
    _cast_weight(w_ref, wb_ref)
    dot = _dot_nt if w_rows_are_outputs else _dot
    o_ref[...] = dot(a_ref[...], wb_ref[...]).astype(o_ref.dtype)


def _matmul(a, w, n, out_dtype, name, w_rows_are_outputs=False):
    m, k = a.shape
    tm = MM_TM if m % MM_TM == 0 else m
    tn = MM_TN
    assert n % tn == 0
    if w_rows_are_outputs:
        w_block, w_spec = (tn, k), pl.BlockSpec((tn, k), lambda j, i: (j, 0))
    else:
        w_block, w_spec = (k, tn), pl.BlockSpec((k, tn), lambda j, i: (0, j))
    return pl.pallas_call(
        functools.partial(_mm_kernel, w_rows_are_outputs=w_rows_are_outputs),
        grid=(n // tn, m // tm),
        in_specs=[pl.BlockSpec((tm, k), lambda j, i: (i, 0)), w_spec],
        out_specs=pl.BlockSpec((tm, tn), lambda j, i: (i, j)),
        out_shape=jax.ShapeDtypeStruct((m, n), out_dtype),
        scratch_shapes=[pltpu.VMEM(w_block, BF16)],
        compiler_params=_cparams(2, 48),
        name=name,
    )(a, w)


def _outproj_kernel(oa_ref, ob_ref, w_ref, x_ref, mod_ref, o_ref, wb_ref):
    _cast_weight(w_ref, wb_ref)
    ka = oa_ref.shape[1]
    acc = _dot(oa_ref[...], wb_ref[0:ka, :]) + _dot(ob_ref[...], wb_ref[ka:, :])
    gate = mod_ref[0, 2:3, :]
    o_ref[...] = x_ref[...] + gate * acc


def _outproj(oa, ob, w, x2d, mod3, seq):
    m, ka = oa.shape
    kb = ob.shape[1]
    k, n = w.shape
    assert ka + kb == k
    tm = min(MM_TM if m % MM_TM == 0 else m, seq)
    tn = MM_TN
    return pl.pallas_call(
        _outproj_kernel,
        grid=(n // tn, m // tm),
        in_specs=[
            pl.BlockSpec((tm, ka), lambda j, i: (i, 0)),
            pl.BlockSpec((tm, kb), lambda j, i: (i, 0)),
            pl.BlockSpec((k, tn), lambda j, i: (0, j)),
            pl.BlockSpec((tm, tn), lambda j, i: (i, j)),
            pl.BlockSpec((1, N_MOD, tn), lambda j, i: ((i * tm) // seq, 0, j)),
        ],
        out_specs=pl.BlockSpec((tm, tn), lambda j, i: (i, j)),
        out_shape=jax.ShapeDtypeStruct((m, n), F32),
        scratch_shapes=[pltpu.VMEM((k, tn), BF16)],
        compiler_params=_cparams(2, 52),
        name="out_proj_residual",
    )(oa, ob, w, x2d, mod3)


def _attn_rows(qa_ref, ka_ref, vt_ref, o_ref, tab_ref):
    seq = qa_ref.shape[0]
    t = ATTN_TILE
    c_log2 = float(HEAD_DIM ** -0.5 * LOG2E)
    row = lax.broadcasted_iota(jnp.int32, (t, t), 0)
    col = lax.broadcasted_iota(jnp.int32, (t, t), 1)
    causal = row <= col
    for qp in range(seq // t):
        kv = (qp + 1) * t
        qa = qa_ref[qp * t:(qp + 1) * t, :]
        st = _dot_nt(ka_ref[0:kv, :], qa) * c_log2
        if tab_ref is not None:
            off = tab_ref.shape[0] - kv
            st = st - tab_ref[off:off + kv, :]
        tail = jnp.where(causal, st[kv - t:kv, :], NEG_INF)
        m = jnp.max(tail, axis=0, keepdims=True)
        if qp > 0:
            head = st[0:kv - t, :]
            m = jnp.maximum(m, jnp.max(head, axis=0, keepdims=True))
            p_head = jnp.exp2(head - m)
        p_tail = jnp.exp2(tail - m)
        l = jnp.sum(p_tail, axis=0, keepdims=True)
        acc = _dot(vt_ref[:, kv - t:kv], p_tail.astype(BF16))
        if qp > 0:
            l = l + jnp.sum(p_head, axis=0, keepdims=True)
            acc = acc + _dot(vt_ref[:, 0:kv - t], p_head.astype(BF16))
        ot = acc * (1.0 / l)
        o_ref[qp * t:(qp + 1) * t, :] = ot.T.astype(o_ref.dtype)


def _moba_kernel(slopes_ref, q_ref, k_ref, v_ref, gq_ref, gk_ref, o_ref,
                 qa_ref, ka_ref, vt_ref, tab_ref):
    h = pl.program_id(1)
    seq = q_ref.shape[0]
    t = ATTN_TILE
    nb = seq // MOBA_BLOCK
    qn = _rms(q_ref[...], gq_ref[...])
    kn = _rms(k_ref[...], gk_ref[...])
    qb16 = qn.astype(BF16)
    kb16 = kn.astype(BF16)

    rid = lax.broadcasted_iota(jnp.int32, (16, HEAD_DIM), 0)
    kmean = jnp.zeros((16, HEAD_DIM), F32)
    for n in range(nb):
        blk = jnp.sum(kn[n * MOBA_BLOCK:(n + 1) * MOBA_BLOCK, :], axis=0, keepdims=True) / float(MOBA_BLOCK)
        kmean = jnp.where(rid == n, blk, kmean)
    gt = _dot_nt(kmean.astype(BF16), qb16)

    bidx = lax.broadcasted_iota(jnp.int32, (16, seq), 0)
    qblk = lax.shift_right_logical(lax.broadcasted_iota(jnp.int32, (16, seq), 1), MOBA_BLOCK_LOG2)
    rank = jnp.zeros((16, seq), F32)
    for n in range(nb):
        gn = gt[n:n + 1, :]
        beats = jnp.where(gn > gt, 1.0, jnp.where(gn == gt, jnp.where(bidx > n, 1.0, 0.0), 0.0))
        rank = rank + jnp.where(qblk > n, beats, 0.0)
    keep = jnp.where(bidx == qblk, 1.0,
                     jnp.where(bidx < qblk, jnp.where(rank < float(MOBA_TOPK), 1.0, 0.0), 0.0))
    bias_t = jnp.where(keep > 0.5, 0.0, -MASK_BIG)
    bias_pad = jnp.concatenate([bias_t, jnp.zeros((HEAD_DIM - 16, seq), F32)], axis=0)
    bias_nat = bias_pad.T

    qa_ref[:, 0:HEAD_DIM] = qb16
    qa_ref[:, HEAD_DIM:2 * HEAD_DIM] = bias_nat.astype(BF16)
    krow = lax.shift_right_logical(lax.broadcasted_iota(jnp.int32, (seq, HEAD_DIM), 0), MOBA_BLOCK_LOG2)
    klane = lax.broadcasted_iota(jnp.int32, (seq, HEAD_DIM), 1)
    ka_ref[:, 0:HEAD_DIM] = kb16
    ka_ref[:, HEAD_DIM:2 * HEAD_DIM] = jnp.where(klane == krow, 1.0, 0.0).astype(BF16)
    vt_ref[...] = v_ref[...].T.astype(BF16)

    slope2 = slopes_ref[h] * LOG2E
    jrow = lax.broadcasted_iota(jnp.int32, (seq, t), 0)
    qcol = lax.broadcasted_iota(jnp.int32, (seq, t), 1)
    tab_ref[...] = slope2 * (qcol - jrow + (seq - t)).astype(F32)

    _attn_rows(qa_ref, ka_ref, vt_ref, o_ref, tab_ref)


def _fox_kernel(q_ref, k_ref, v_ref, gq_ref, gk_ref, fp_ref, o_ref, qa_ref, ka_ref, vt_ref):
    h = pl.program_id(1)
    qn = _rms(q_ref[...], gq_ref[...])
    kn = _rms(k_ref[...], gk_ref[...])
    packed = fp_ref[0]
    r = lax.broadcasted_iota(jnp.int32, (128, 128), 0)
    c = lax.broadcasted_iota(jnp.int32, (128, 128), 1)
    ones_row = r == 3 * FG_GROUP
    part = jnp.where(r == h, 0, jnp.where(r == FG_GROUP + h, 1, jnp.where(r == 2 * FG_GROUP + h, 2, -1)))
    pq = jnp.where(part == c, 1.0, jnp.where(ones_row & (c >= 3) & (c < 6), 1.0, 0.0)).astype(BF16)
    pk = jnp.where((part + 3 == c) & (part >= 0), -1.0, jnp.where(ones_row & (c < 3), 1.0, 0.0)).astype(BF16)
    qa_ref[:, 0:HEAD_DIM] = qn.astype(BF16)
    qa_ref[:, HEAD_DIM:2 * HEAD_DIM] = _dot(packed, pq).astype(BF16)
    ka_ref[:, 0:HEAD_DIM] = kn.astype(BF16)
    ka_ref[:, HEAD_DIM:2 * HEAD_DIM] = _dot(packed, pk).astype(BF16)
    vt_ref[...] = v_ref[...].T.astype(BF16)
    _attn_rows(qa_ref, ka_ref, vt_ref, o_ref, None)


def _attn_scratch(seq):
    return [
        pltpu.VMEM((seq, 2 * HEAD_DIM), BF16),
        pltpu.VMEM((seq, 2 * HEAD_DIM), BF16),
        pltpu.VMEM((HEAD_DIM, seq), BF16),
    ]


def _moba(proj, slopes, gq, gk, batch, seq, n_heads, col0):
    hspec = lambda off: pl.BlockSpec((seq, HEAD_DIM), lambda b, h: (b, col0 + off + h))
    return pl.pallas_call(
        _moba_kernel,
        grid=(batch, n_heads),
        in_specs=[
            pl.BlockSpec(memory_space=pltpu.SMEM),
            hspec(0), hspec(n_heads), hspec(2 * n_heads),
            pl.BlockSpec((1, HEAD_DIM), lambda b, h: (0, 0)),
            pl.BlockSpec((1, HEAD_DIM), lambda b, h: (0, 0)),
        ],
        out_specs=pl.BlockSpec((seq, HEAD_DIM), lambda b, h: (b, h)),
        out_shape=jax.ShapeDtypeStruct((batch * seq, n_heads * HEAD_DIM), BF16),
        scratch_shapes=_attn_scratch(seq) + [pltpu.VMEM((seq, ATTN_TILE), F32)],
        compiler_params=_cparams(2, 40),
        name="moba_attention",
    )(slopes, proj, proj, proj, gq, gk)


def _fox(proj, fpack, gq, gk, batch, seq, n_heads, col0):
    hspec = lambda off: pl.BlockSpec((seq, HEAD_DIM), lambda b, h: (b, col0 + off + h))
    return pl.pallas_call(
        _fox_kernel,
        grid=(batch, n_heads),
        in_specs=[
            hspec(0), hspec(n_heads), hspec(2 * n_heads),
            pl.BlockSpec((1, HEAD_DIM), lambda b, h: (0, 0)),
            pl.BlockSpec((1, HEAD_DIM), lambda b, h: (0, 0)),
            pl.BlockSpec((1, seq, 128), lambda b, h: (b, 0, 0)),
        ],
        out_specs=pl.BlockSpec((seq, HEAD_DIM), lambda b, h: (b, h)),
        out_shape=jax.ShapeDtypeStruct((batch * seq, n_heads * HEAD_DIM), BF16),
        scratch_shapes=_attn_scratch(seq),
        compiler_params=_cparams(2, 40),
        name="fox_attention",
    )(proj, proj, proj, gq, gk, fpack)


def _top16_sorted(st):
    tt = st.shape[1]
    rid = lax.broadcasted_iota(jnp.int32, (PEER_TOPK, tt), 0)
    sv = jnp.zeros((PEER_TOPK, tt), F32)
    work = st
    for r in range(PEER_TOPK):
        mx = jnp.max(work, axis=0, keepdims=True)
        sv = jnp.where(rid == r, mx, sv)
        if r + 1 < PEER_TOPK:
            work = jnp.where(work == mx, -jnp.inf, work)
    return sv


def _peer_select_kernel(qp_ref, keys_ref, w_ref, s2_s, e2_s, th_s, c1_s):
    tt = qp_ref.shape[0]
    n_lc = tt // 128
    b8 = lax.broadcasted_iota(jnp.int32, (8, tt), 0)
    inf = jnp.inf
    for h in range(PEER_HEADS):
        q1 = qp_ref[:, (2 * h) * PEER_HALF:(2 * h + 1) * PEER_HALF].astype(BF16)
        q2 = qp_ref[:, (2 * h + 1) * PEER_HALF:(2 * h + 2) * PEER_HALF].astype(BF16)
        s1 = _dot_nt(keys_ref[2 * h], q1)
        s2 = _dot_nt(keys_ref[2 * h + 1], q2)
        sv0 = _top16_sorted(s1)
        sv1 = _top16_sorted(s2)
        sv1a, sv1b = sv1[0:8, :], sv1[8:16, :]
        row = lambda a: sv0[a:a + 1, :]
        cands = [(row(0) + sv1a, sv1a), (row(0) + sv1b, sv1b), (row(1) + sv1a, sv1a)]
        for a, nbv in ((2, 5), (3, 4), (4, 3), (5, 2), (6, 2), (7, 2)):
            cands.append((jnp.where(b8 < nbv, row(a) + sv1a, -inf), sv1a))
        top1 = sv1[0:1, :]
        hsum = sv0[8:16, :] + top1
        work = [g for g, _ in cands] + [hsum]
        tau = None
        for r in range(PEER_TOPK):
            mx = functools.reduce(jnp.maximum, work)
            tau = jnp.max(mx, axis=0, keepdims=True)
            if r + 1 < PEER_TOPK:
                work = [jnp.where(g == tau, -inf, g) for g in work]
        top = row(0) + top1
        zsum = jnp.zeros((1, tt), F32)
        for g in [g for g, _ in cands] + [hsum]:
            zsum = zsum + jnp.sum(jnp.where(g >= tau, jnp.exp(g - top), 0.0), axis=0, keepdims=True)
        inv_z = 1.0 / zsum
        th_rows = []
        th0 = jnp.minimum(jnp.min(jnp.where(cands[0][0] >= tau, sv1a, inf), axis=0, keepdims=True),
                          jnp.min(jnp.where(cands[1][0] >= tau, sv1b, inf), axis=0, keepdims=True))
        th_rows.append(th0)
        for g, s in cands[2:]:
            th_rows.append(jnp.min(jnp.where(g >= tau, s, inf), axis=0, keepdims=True))
        th_hi = jnp.where(hsum >= tau, top1, inf)
        th = jnp.full(s1.shape, inf, F32)
        for a in range(PEER_TOPK):
            ta = th_rows[a] if a < 8 else th_hi[a - 8:a - 7, :]
            th = jnp.where(s1 == row(a), ta, th)
        e2 = jnp.exp(s2 - top1)
        c1 = jnp.exp(s1 - row(0)) * inv_z
        for lc in range(n_lc):
            cols = slice(lc * 128, (lc + 1) * 128)
            s2_s[h, lc] = s2[:, cols]
            e2_s[h, lc] = e2[:, cols]
            th_s[h, lc] = th[:, cols]
            c1_s[h, lc] = c1[:, cols]

    def gate_rows(i, carry):
        for lc in range(n_lc):
            acc = None
            for h in range(PEER_HEADS):
                th = th_s[h, lc, pl.ds(i, 1), :]
                c1 = c1_s[h, lc, pl.ds(i, 1), :]
                w = jnp.where(s2_s[h, lc] >= th, e2_s[h, lc] * c1, 0.0)
                acc = w if acc is None else acc + w
            w_ref[pl.ds(pl.multiple_of(i * PEER_NKEYS, PEER_NKEYS), PEER_NKEYS), lc * 128:(lc + 1) * 128] = acc.astype(BF16)
        return carry

    lax.fori_loop(0, PEER_NKEYS, gate_rows, 0)


def _peer_select(qp, keys16):
    t = qp.shape[0]
    tt = 256
    n_exp = PEER_NKEYS * PEER_NKEYS
    stage = pltpu.VMEM((PEER_HEADS, tt // 128, PEER_NKEYS, 128), F32)
    return pl.pallas_call(
        _peer_select_kernel,
        grid=(t // tt,),
        in_specs=[
            pl.BlockSpec((tt, qp.shape[1]), lambda i: (i, 0)),
            pl.BlockSpec(keys16.shape, lambda i: (0, 0, 0)),
        ],
        out_specs=pl.BlockSpec((n_exp, tt), lambda i: (0, i)),
        out_shape=jax.ShapeDtypeStruct((n_exp, t), BF16),
        scratch_shapes=[stage, stage, stage, stage],
        compiler_params=_cparams(1, 48),
        name="peer_select",
    )(qp, keys16)


def _peer_dense_kernel(h_ref, u_ref, v_ref, w_ref, x_ref, mod_ref, o_ref):
    e = pl.program_id(1)

    @pl.when(e == 0)
    def _():
        o_ref[...] = jnp.zeros_like(o_ref)

    at = _dot_nt(u_ref[...], h_ref[...])
    gelu = 0.5 * at * (1.0 + lax.erf(at * float(np.sqrt(0.5))))
    wt = (w_ref[...].astype(F32) * gelu).astype(BF16)
    o_ref[...] += _dot_tn(wt, v_ref[...])

    @pl.when(e == pl.num_programs(1) - 1)
    def _():
        o_ref[...] = x_ref[...] + mod_ref[0, 5:6, :] * o_ref[...]


def _peer_dense(h2, u16, v16, wgt, x1, mod3, seq):
    t, d = h2.shape
    n_exp = u16.shape[0]
    tm = 512 if t % 512 == 0 else t
    tm = min(tm, seq)
    te = 512
    return pl.pallas_call(
        _peer_dense_kernel,
        grid=(t // tm, n_exp // te),
        in_specs=[
            pl.BlockSpec((tm, d), lambda i, e: (i, 0)),
            pl.BlockSpec((te, d), lambda i, e: (e, 0)),
            pl.BlockSpec((te, d), lambda i, e: (e, 0)),
            pl.BlockSpec((te, tm), lambda i, e: (e, i)),
            pl.BlockSpec((tm, d), lambda i, e: (i, 0), pipeline_mode=pl.Buffered(1)),
            pl.BlockSpec((1, N_MOD, d), lambda i, e: ((i * tm) // seq, 0, 0)),
        ],
        out_specs=pl.BlockSpec((tm, d), lambda i, e: (i, 0)),
        out_shape=jax.ShapeDtypeStruct((t, d), F32),
        compiler_params=_cparams(2, 56),
        name="peer_dense",
    )(h2, u16, v16, wgt, x1, mod3)


def _layer(x, c8, w_ada, b_ada, norm1_g, w_in, b_f, qn_m, kn_m, qn_f, kn_f, w_out, norm2_g,
           w_pq, sub_keys, peer_u, peer_v):
    b, s, d = x.shape
    t = b * s
    n_heads = d // HEAD_DIM
    n_moba = n_heads // 2
    n_fox = n_heads - n_moba
    d_moba = n_moba * HEAD_DIM
    d_fox = n_fox * HEAD_DIM
    d_qkv = 3 * d_moba + 3 * d_fox
    assert s % ATTN_TILE == 0 and ATTN_TILE % MOBA_BLOCK == 0 and s % 512 == 0
    assert n_fox <= FG_GROUP and s // MOBA_BLOCK <= 16 and b <= 8

    mod = _adaln(c8, w_ada, b_ada.reshape(1, -1))
    mod3 = mod[:b].reshape(b, N_MOD, d)

    w_in_t = jnp.swapaxes(w_in, 0, 1)
    w_fg_t = w_in_t[d_qkv:]
    wfg3 = jnp.zeros((128, d), F32)
    bfg3 = jnp.zeros((1, 128), F32)
    for g in range(3):
        wfg3 = wfg3.at[g * FG_GROUP:g * FG_GROUP + n_fox, :].set(w_fg_t)
        bfg3 = bfg3.at[0, g * FG_GROUP:g * FG_GROUP + n_fox].set(b_f)
    h1, fpack = _norm1(x, norm1_g.reshape(1, d), mod3, wfg3.astype(BF16), bfg3, n_fox)

    proj = _matmul(h1.reshape(t, d), w_in_t, d_qkv, F32, "in_proj", w_rows_are_outputs=True)

    slopes = 2.0 ** (-8.0 * jnp.arange(1, n_moba + 1, dtype=F32) / n_moba)
    o_moba = _moba(proj, slopes, qn_m.reshape(1, -1), kn_m.reshape(1, -1), b, s, n_moba, 0)
    o_fox = _fox(proj, fpack, qn_f.reshape(1, -1), kn_f.reshape(1, -1), b, s, n_fox, 3 * n_moba)

    x1 = _outproj(o_moba, o_fox, w_out, x.reshape(t, d), mod3, s).reshape(b, s, d)

    h2 = _norm2(x1, norm2_g.reshape(1, d), mod3).reshape(t, d)
    qp = _matmul(h2, w_pq, w_pq.shape[1], F32, "peer_query")
    keys16 = sub_keys.reshape(2 * PEER_HEADS, PEER_NKEYS, PEER_HALF).astype(BF16)
    wgt = _peer_select(qp, keys16)
    out = _peer_dense(h2, peer_u.astype(BF16), peer_v.astype(BF16), wgt, x1.reshape(t, d), mod3, s)
    return out.reshape(b, s, d)


def kernel(x, c, w_ada, b_ada, norm1_g, w_in, b_f, q_norm_moba, k_norm_moba, q_norm_fox, k_norm_fox,
           w_out, norm2_g, w_pq, peer_sub_keys, peer_u, peer_v):
    b = x.shape[0]
    c8 = jnp.zeros((8, c.shape[1]), F32).at[:b].set(c)
    for l in range(w_ada.shape[0]):
        x = _layer(x, c8, w_ada[l], b_ada[l], norm1_g[l], w_in[l], b_f[l], q_norm_moba[l], k_norm_moba[l],
                   q_norm_fox[l], k_norm_fox[l], w_out[l], norm2_g[l], w_pq[l], peer_sub_keys[l],
                   peer_u[l], peer_v[l])
    return x
```

```python
import functools

import jax
import jax.numpy as jnp
import numpy as np
from jax import lax
from jax.experimental import pallas as pl
from jax.experimental.pallas import tpu as pltpu

F32 = jnp.float32
BF16 = jnp.bfloat16

HEAD_DIM = 128
MOBA_BLOCK = 256
MOBA_BLOCK_LOG2 = 8
MOBA_TOPK = 3
PEER_HEADS = 8
PEER_NKEYS = 128
PEER_TOPK = 16
PEER_HALF = 128
N_MOD = 6
NORM_EPS = 1e-6
NEG_INF = -1e30
MASK_BIG = 2.0 ** 100
ATTN_TILE = 512
LOG2E = 1.4426950408889634
FG_GROUP = 16

MIB = 1024 * 1024


def _cparams(n_axes, vmem_mib):
    return pltpu.CompilerParams(
        dimension_semantics=("arbitrary",) * n_axes,
        vmem_limit_bytes=int(vmem_mib * MIB),
    )


def _dot(a, b):
    return jnp.dot(a, b, preferred_element_type=F32)


def _dot_nt(a, b):
    return lax.dot_general(a, b, (((1,), (1,)), ((), ())), preferred_element_type=F32)


def _dot_tn(a, b):
    return lax.dot_general(a, b, (((0,), (0,)), ((), ())), preferred_element_type=F32)


def _split3(x):
    p1 = x.astype(BF16)
    r1 = x - p1.astype(F32)
    p2 = r1.astype(BF16)
    r2 = r1 - p2.astype(F32)
    p3 = r2.astype(BF16)
    return p1, p2, p3


def _rms(x, g):
    ms = jnp.mean(x * x, axis=-1, keepdims=True)
    return x * lax.rsqrt(ms + NORM_EPS) * g


def _adaln_kernel(c_ref, w_ref, b_ref, o_ref):
    c = c_ref[...]
    ca = c * (1.0 / (1.0 + jnp.exp(-c)))
    o_ref[...] = _dot(ca.astype(BF16), w_ref[...].astype(BF16)) + b_ref[...]


def _adaln(c_pad, w, b_row):
    rows, d = c_pad.shape
    n = w.shape[1]
    tn = 512
    return pl.pallas_call(
        _adaln_kernel,
        grid=(n // tn,),
        in_specs=[
            pl.BlockSpec((rows, d), lambda j: (0, 0)),
            pl.BlockSpec((d, tn), lambda j: (0, j)),
            pl.BlockSpec((1, tn), lambda j: (0, j)),
        ],
        out_specs=pl.BlockSpec((rows, tn), lambda j: (0, j)),
        out_shape=jax.ShapeDtypeStruct((rows, n), F32),
        compiler_params=_cparams(1, 40),
        name="adaln",
    )(c_pad, w, b_row)


def _norm_mod(x_ref, g_ref, mod_ref, row0):
    xf = x_ref[0]
    y = _rms(xf, g_ref[...])
    sh = mod_ref[0, row0:row0 + 1, :]
    sc = mod_ref[0, row0 + 1:row0 + 2, :]
    return y * (1.0 + sc) + sh


def _norm1_kernel(x_ref, g_ref, mod_ref, wfg_ref, bfg_ref, h_ref, fp_ref, carry_ref, *, inv_scale, n_fox):
    s_idx = pl.program_id(1)
    hb = _norm_mod(x_ref, g_ref, mod_ref, 0).astype(BF16)
    h_ref[0] = hb
    ts = hb.shape[0]

    @pl.when(s_idx == 0)
    def _():
        carry_ref[...] = jnp.zeros_like(carry_ref)

    z = _dot_nt(hb, wfg_ref[...]) + bfg_ref[...]
    lf = jnp.minimum(z, 0.0) - jnp.log1p(jnp.exp(-jnp.abs(z)))
    r = lax.broadcasted_iota(jnp.int32, (ts, ts), 0)
    c = lax.broadcasted_iota(jnp.int32, (ts, ts), 1)
    tri = jnp.where(c <= r, 1.0, 0.0).astype(BF16)
    l1, l2, l3 = _split3(lf)
    f = (_dot(tri, l3) + _dot(tri, l2)) + _dot(tri, l1) + carry_ref[...]
    carry_ref[...] = f[ts - 1:ts, :]
    f1, f2, f3 = [p.astype(F32) for p in _split3(f * inv_scale)]
    lane = lax.broadcasted_iota(jnp.int32, f.shape, 1)
    packed = jnp.where(lane < FG_GROUP, f1,
              jnp.where(lane < 2 * FG_GROUP, f2,
               jnp.where(lane < 3 * FG_GROUP, f3,
                jnp.where(lane == 3 * FG_GROUP, 1.0, 0.0))))
    fp_ref[0] = packed.astype(BF16)


def _norm1(x, g_row, mod3, wfg3, bfg3, n_fox):
    b, s, d = x.shape
    ts = 512
    kern = functools.partial(_norm1_kernel, inv_scale=float(HEAD_DIM ** 0.5), n_fox=n_fox)
    return pl.pallas_call(
        kern,
        grid=(b, s // ts),
        in_specs=[
            pl.BlockSpec((1, ts, d), lambda i, j: (i, j, 0)),
            pl.BlockSpec((1, d), lambda i, j: (0, 0)),
            pl.BlockSpec((1, N_MOD, d), lambda i, j: (i, 0, 0)),
            pl.BlockSpec((128, d), lambda i, j: (0, 0)),
            pl.BlockSpec((1, 128), lambda i, j: (0, 0)),
        ],
        out_specs=[
            pl.BlockSpec((1, ts, d), lambda i, j: (i, j, 0)),
            pl.BlockSpec((1, ts, 128), lambda i, j: (i, j, 0)),
        ],
        out_shape=[
            jax.ShapeDtypeStruct((b, s, d), BF16),
            jax.ShapeDtypeStruct((b, s, 128), BF16),
        ],
        scratch_shapes=[pltpu.VMEM((1, 128), F32)],
        compiler_params=_cparams(2, 48),
        name="norm1_fgate",
    )(x, g_row, mod3, wfg3, bfg3)


def _norm2_kernel(x_ref, g_ref, mod_ref, h_ref):
    h_ref[0] = _norm_mod(x_ref, g_ref, mod_ref, 3).astype(BF16)


def _norm2(x, g_row, mod3):
    b, s, d = x.shape
    ts = 512
    return pl.pallas_call(
        _norm2_kernel,
        grid=(b, s // ts),
        in_specs=[
            pl.BlockSpec((1, ts, d), lambda i, j: (i, j, 0)),
            pl.BlockSpec((1, d), lambda i, j: (0, 0)),
            pl.BlockSpec((1, N_MOD, d), lambda i, j: (i, 0, 0)),
        ],
        out_specs=pl.BlockSpec((1, ts, d), lambda i, j: (i, j, 0)),
        out_shape=jax.ShapeDtypeStruct((b, s, d), BF16),
        compiler_params=_cparams(2, 48),
        name="norm2",
    )(x, g_row, mod3)


MM_TM = 1024
MM_TN = 512


def _cast_weight(w_ref, wb_ref):
    @pl.when(pl.program_id(1) == 0)
    def _():
        wb_ref[...] = w_ref[...].astype(BF16)


def _mm_kernel(a_ref, w_ref, o_ref, wb_ref, *, w_rows_are_outputs):
    _cast_weight(w_ref, wb_ref)
    dot = _dot_nt if w_rows_are_outputs else _dot
    o_ref[...] = dot(a_ref[...], wb_ref[...]).astype(o_ref.dtype)


def _matmul(a, w, n, out_dtype, name, w_rows_are_outputs=False):
    m, k = a.shape
    tm = MM_TM if m % MM_TM == 0 else m
    tn = MM_TN
    assert n % tn == 0
    if w_rows_are_outputs:
        w_block, w_spec = (tn, k), pl.BlockSpec((tn, k), lambda j, i: (j, 0))
    else:
        w_block, w_spec = (k, tn), pl.BlockSpec((k, tn), lambda j, i: (0, j))
    return pl.pallas_call(
        functools.partial(_mm_kernel, w_rows_are_outputs=w_rows_are_outputs),
        grid=(n // tn, m // tm),
        in_specs=[pl.BlockSpec((tm, k), lambda j, i: (i, 0)), w_spec],
        out_specs=pl.BlockSpec((tm, tn), lambda j, i: (i, j)),
        out_shape=jax.ShapeDtypeStruct((m, n), out_dtype),
        scratch_shapes=[pltpu.VMEM(w_block, BF16)],
        compiler_params=_cparams(2, 48),
        name=name,
    )(a, w)


def _outproj_kernel(oa_ref, ob_ref, w_ref, x_ref, mod_ref, o_ref, wb_ref):
    _cast_weight(w_ref, wb_ref)
    ka = oa_ref.shape[1]
    acc = _dot(oa_ref[...], wb_ref[0:ka, :]) + _dot(ob_ref[...], wb_ref[ka:, :])
    gate = mod_ref[0, 2:3, :]
    o_ref[...] = x_ref[...] + gate * acc


def _outproj(oa, ob, w, x2d, mod3, seq):
    m, ka = oa.shape
    kb = ob.shape[1]
    k, n = w.shape
    assert ka + kb == k
    tm = min(MM_TM if m % MM_TM == 0 else m, seq)
    tn = MM_TN
    return pl.pallas_call(
        _outproj_kernel,
        grid=(n // tn, m // tm),
        in_specs=[
            pl.BlockSpec((tm, ka), lambda j, i: (i, 0)),
            pl.BlockSpec((tm, kb), lambda j, i: (i, 0)),
            pl.BlockSpec((k, tn), lambda j, i: (0, j)),
            pl.BlockSpec((tm, tn), lambda j, i: (i, j)),
            pl.BlockSpec((1, N_MOD, tn), lambda j, i: ((i * tm) // seq, 0, j)),
        ],
        out_specs=pl.BlockSpec((tm, tn), lambda j, i: (i, j)),
        out_shape=jax.ShapeDtypeStruct((m, n), F32),
        scratch_shapes=[pltpu.VMEM((k, tn), BF16)],
        compiler_params=_cparams(2, 52),
        name="out_proj_residual",
    )(oa, ob, w, x2d, mod3)


def _attn_rows(qa_ref, ka_ref, vt_ref, o_ref, tab_ref):
    seq = qa_ref.shape[0]
    t = ATTN_TILE
    c_log2 = float(HEAD_DIM ** -0.5 * LOG2E)
    row = lax.broadcasted_iota(jnp.int32, (t, t), 0)
    col = lax.broadcasted_iota(jnp.int32, (t, t), 1)
    causal = row <= col
    for qp in range(seq // t):
        kv = (qp + 1) * t
        qa = qa_ref[qp * t:(qp + 1) * t, :]
        st = _dot_nt(ka_ref[0:kv, :], qa) * c_log2
        if tab_ref is not None:
            off = tab_ref.shape[0] - kv
            st = st - tab_ref[off:off + kv, :]
        tail = jnp.where(causal, st[kv - t:kv, :], NEG_INF)
        m = jnp.max(tail, axis=0, keepdims=True)
        if qp > 0:
            head = st[0:kv - t, :]
            m = jnp.maximum(m, jnp.max(head, axis=0, keepdims=True))
            p_head = jnp.exp2(head - m)
        p_tail = jnp.exp2(tail - m)
        l = jnp.sum(p_tail, axis=0, keepdims=True)
        acc = _dot(vt_ref[:, kv - t:kv], p_tail.astype(BF16))
        if qp > 0:
            l = l + jnp.sum(p_head, axis=0, keepdims=True)
            acc = acc + _dot(vt_ref[:, 0:kv - t], p_head.astype(BF16))
        ot = acc * (1.0 / l)
        o_ref[qp * t:(qp + 1) * t, :] = ot.T.astype(o_ref.dtype)


def _moba_kernel(slopes_ref, q_ref, k_ref, v_ref, gq_ref, gk_ref, wsrc_ref, o_ref, wdst_ref,
                 qa_ref, ka_ref, vt_ref, tab_ref):
    h = pl.program_id(1)
    wdst_ref[...] = wsrc_ref[...].astype(BF16)
    seq = q_ref.shape[0]
    t = ATTN_TILE
    nb = seq // MOBA_BLOCK
    qn = _rms(q_ref[...], gq_ref[...])
    kn = _rms(k_ref[...], gk_ref[...])
    qb16 = qn.astype(BF16)
    kb16 = kn.astype(BF16)

    rid = lax.broadcasted_iota(jnp.int32, (16, HEAD_DIM), 0)
    kmean = jnp.zeros((16, HEAD_DIM), F32)
    for n in range(nb):
        blk = jnp.sum(kn[n * MOBA_BLOCK:(n + 1) * MOBA_BLOCK, :], axis=0, keepdims=True) / float(MOBA_BLOCK)
        kmean = jnp.where(rid == n, blk, kmean)
    gt = _dot_nt(kmean.astype(BF16), qb16)

    bidx = lax.broadcasted_iota(jnp.int32, (16, seq), 0)
    qblk = lax.shift_right_logical(lax.broadcasted_iota(jnp.int32, (16, seq), 1), MOBA_BLOCK_LOG2)
    rank = jnp.zeros((16, seq), F32)
    for n in range(nb):
        gn = gt[n:n + 1, :]
        beats = jnp.where(gn > gt, 1.0, jnp.where(gn == gt, jnp.where(bidx > n, 1.0, 0.0), 0.0))
        rank = rank + jnp.where(qblk > n, beats, 0.0)
    keep = jnp.where(bidx == qblk, 1.0,
                     jnp.where(bidx < qblk, jnp.where(rank < float(MOBA_TOPK), 1.0, 0.0), 0.0))
    bias_t = jnp.where(keep > 0.5, 0.0, -MASK_BIG)
    bias_pad = jnp.concatenate([bias_t, jnp.zeros((HEAD_DIM - 16, seq), F32)], axis=0)
    bias_nat = bias_pad.T

    qa_ref[:, 0:HEAD_DIM] = qb16
    qa_ref[:, HEAD_DIM:2 * HEAD_DIM] = bias_nat.astype(BF16)
    krow = lax.shift_right_logical(lax.broadcasted_iota(jnp.int32, (seq, HEAD_DIM), 0), MOBA_BLOCK_LOG2)
    klane = lax.broadcasted_iota(jnp.int32, (seq, HEAD_DIM), 1)
    ka_ref[:, 0:HEAD_DIM] = kb16
    ka_ref[:, HEAD_DIM:2 * HEAD_DIM] = jnp.where(klane == krow, 1.0, 0.0).astype(BF16)
    vt_ref[...] = v_ref[...].T.astype(BF16)

    slope2 = slopes_ref[h] * LOG2E
    jrow = lax.broadcasted_iota(jnp.int32, (seq, t), 0)
    qcol = lax.broadcasted_iota(jnp.int32, (seq, t), 1)
    tab_ref[...] = slope2 * (qcol - jrow + (seq - t)).astype(F32)

    _attn_rows(qa_ref, ka_ref, vt_ref, o_ref, tab_ref)


def _fox_kernel(q_ref, k_ref, v_ref, gq_ref, gk_ref, fp_ref, wsrc_ref, o_ref, wdst_ref,
                qa_ref, ka_ref, vt_ref):
    h = pl.program_id(1)
    wdst_ref[...] = wsrc_ref[...].astype(BF16)
    qn = _rms(q_ref[...], gq_ref[...])
    kn = _rms(k_ref[...], gk_ref[...])
    packed = fp_ref[0]
    r = lax.broadcasted_iota(jnp.int32, (128, 128), 0)
    c = lax.broadcasted_iota(jnp.int32, (128, 128), 1)
    ones_row = r == 3 * FG_GROUP
    part = jnp.where(r == h, 0, jnp.where(r == FG_GROUP + h, 1, jnp.where(r == 2 * FG_GROUP + h, 2, -1)))
    pq = jnp.where(part == c, 1.0, jnp.where(ones_row & (c >= 3) & (c < 6), 1.0, 0.0)).astype(BF16)
    pk = jnp.where((part + 3 == c) & (part >= 0), -1.0, jnp.where(ones_row & (c < 3), 1.0, 0.0)).astype(BF16)
    qa_ref[:, 0:HEAD_DIM] = qn.astype(BF16)
    qa_ref[:, HEAD_DIM:2 * HEAD_DIM] = _dot(packed, pq).astype(BF16)
    ka_ref[:, 0:HEAD_DIM] = kn.astype(BF16)
    ka_ref[:, HEAD_DIM:2 * HEAD_DIM] = _dot(packed, pk).astype(BF16)
    vt_ref[...] = v_ref[...].T.astype(BF16)
    _attn_rows(qa_ref, ka_ref, vt_ref, o_ref, None)


def _attn_scratch(seq):
    return [
        pltpu.VMEM((seq, 2 * HEAD_DIM), BF16),
        pltpu.VMEM((seq, 2 * HEAD_DIM), BF16),
        pltpu.VMEM((HEAD_DIM, seq), BF16),
    ]


def _cast_slab_specs(w, n_steps, step_of):
    rows, cols = w.shape
    slab = rows // n_steps
    assert slab * n_steps == rows and slab % 16 == 0
    spec = pl.BlockSpec((slab, cols), lambda *g: (step_of(*g), 0))
    return spec, spec, jax.ShapeDtypeStruct((rows, cols), BF16)


def _moba(proj, slopes, gq, gk, batch, seq, n_heads, col0, w_f32):
    hspec = lambda off: pl.BlockSpec((seq, HEAD_DIM), lambda b, h: (b, col0 + off + h))
    w_in_spec, w_out_spec, w_shape = _cast_slab_specs(w_f32, n_heads * batch, lambda b, h: b * n_heads + h)
    return pl.pallas_call(
        _moba_kernel,
        grid=(batch, n_heads),
        in_specs=[
            pl.BlockSpec(memory_space=pltpu.SMEM),
            hspec(0), hspec(n_heads), hspec(2 * n_heads),
            pl.BlockSpec((1, HEAD_DIM), lambda b, h: (0, 0)),
            pl.BlockSpec((1, HEAD_DIM), lambda b, h: (0, 0)),
            w_in_spec,
        ],
        out_specs=[pl.BlockSpec((seq, HEAD_DIM), lambda b, h: (b, h)), w_out_spec],
        out_shape=[jax.ShapeDtypeStruct((batch * seq, n_heads * HEAD_DIM), BF16), w_shape],
        scratch_shapes=_attn_scratch(seq) + [pltpu.VMEM((seq, ATTN_TILE), F32)],
        compiler_params=_cparams(2, 56),
        name="moba_attention",
    )(slopes, proj, proj, proj, gq, gk, w_f32)


def _fox(proj, fpack, gq, gk, batch, seq, n_heads, col0, w_f32):
    hspec = lambda off: pl.BlockSpec((seq, HEAD_DIM), lambda b, h: (b, col0 + off + h))
    w_in_spec, w_out_spec, w_shape = _cast_slab_specs(w_f32, n_heads * batch, lambda b, h: b * n_heads + h)
    return pl.pallas_call(
        _fox_kernel,
        grid=(batch, n_heads),
        in_specs=[
            hspec(0), hspec(n_heads), hspec(2 * n_heads),
            pl.BlockSpec((1, HEAD_DIM), lambda b, h: (0, 0)),
            pl.BlockSpec((1, HEAD_DIM), lambda b, h: (0, 0)),
            pl.BlockSpec((1, seq, 128), lambda b, h: (b, 0, 0)),
            w_in_spec,
        ],
        out_specs=[pl.BlockSpec((seq, HEAD_DIM), lambda b, h: (b, h)), w_out_spec],
        out_shape=[jax.ShapeDtypeStruct((batch * seq, n_heads * HEAD_DIM), BF16), w_shape],
        scratch_shapes=_attn_scratch(seq),
        compiler_params=_cparams(2, 56),
        name="fox_attention",
    )(proj, proj, proj, gq, gk, fpack, w_f32)


def _top16_sorted(st):
    tt = st.shape[1]
    rid = lax.broadcasted_iota(jnp.int32, (PEER_TOPK, tt), 0)
    sv = jnp.zeros((PEER_TOPK, tt), F32)
    work = st
    for r in range(PEER_TOPK):
        mx = jnp.max(work, axis=0, keepdims=True)
        sv = jnp.where(rid == r, mx, sv)
        if r + 1 < PEER_TOPK:
            work = jnp.where(work == mx, -jnp.inf, work)
    return sv


def _peer_select_kernel(qp_ref, keys_ref, w_ref, s2_s, e2_s, th_s, c1_s):
    tt = qp_ref.shape[0]
    n_lc = tt // 128
    b8 = lax.broadcasted_iota(jnp.int32, (8, tt), 0)
    inf = jnp.inf
    for h in range(PEER_HEADS):
        q1 = qp_ref[:, (2 * h) * PEER_HALF:(2 * h + 1) * PEER_HALF].astype(BF16)
        q2 = qp_ref[:, (2 * h + 1) * PEER_HALF:(2 * h + 2) * PEER_HALF].astype(BF16)
        s1 = _dot_nt(keys_ref[2 * h], q1)
        s2 = _dot_nt(keys_ref[2 * h + 1], q2)
        sv0 = _top16_sorted(s1)
        sv1 = _top16_sorted(s2)
        sv1a, sv1b = sv1[0:8, :], sv1[8:16, :]
        row = lambda a: sv0[a:a + 1, :]
        cands = [(row(0) + sv1a, sv1a), (row(0) + sv1b, sv1b), (row(1) + sv1a, sv1a)]
        for a, nbv in ((2, 5), (3, 4), (4, 3), (5, 2), (6, 2), (7, 2)):
            cands.append((jnp.where(b8 < nbv, row(a) + sv1a, -inf), sv1a))
        top1 = sv1[0:1, :]
        hsum = sv0[8:16, :] + top1
        work = [g for g, _ in cands] + [hsum]
        tau = None
        for r in range(PEER_TOPK):
            mx = functools.reduce(jnp.maximum, work)
            tau = jnp.max(mx, axis=0, keepdims=True)
            if r + 1 < PEER_TOPK:
                work = [jnp.where(g == tau, -inf, g) for g in work]
        top = row(0) + top1
        zsum = jnp.zeros((1, tt), F32)
        for g in [g for g, _ in cands] + [hsum]:
            zsum = zsum + jnp.sum(jnp.where(g >= tau, jnp.exp(g - top), 0.0), axis=0, keepdims=True)
        inv_z = 1.0 / zsum
        th_rows = []
        th0 = jnp.minimum(jnp.min(jnp.where(cands[0][0] >= tau, sv1a, inf), axis=0, keepdims=True),
                          jnp.min(jnp.where(cands[1][0] >= tau, sv1b, inf), axis=0, keepdims=True))
        th_rows.append(th0)
        for g, s in cands[2:]:
            th_rows.append(jnp.min(jnp.where(g >= tau, s, inf), axis=0, keepdims=True))
        th_hi = jnp.where(hsum >= tau, top1, inf)
        th = jnp.full(s1.shape, inf, F32)
        for a in range(PEER_TOPK):
            ta = th_rows[a] if a < 8 else th_hi[a - 8:a - 7, :]
            th = jnp.where(s1 == row(a), ta, th)
        e2 = jnp.exp(s2 - top1)
        c1 = jnp.exp(s1 - row(0)) * inv_z
        for lc in range(n_lc):
            cols = slice(lc * 128, (lc + 1) * 128)
            s2_s[h, lc] = s2[:, cols]
            e2_s[h, lc] = e2[:, cols]
            th_s[h, lc] = th[:, cols]
            c1_s[h, lc] = c1[:, cols]

    def gate_rows(i, carry):
        for lc in range(n_lc):
            acc = None
            for h in range(PEER_HEADS):
                th = th_s[h, lc, pl.ds(i, 1), :]
                c1 = c1_s[h, lc, pl.ds(i, 1), :]
                w = jnp.where(s2_s[h, lc] >= th, e2_s[h, lc] * c1, 0.0)
                acc = w if acc is None else acc + w
            w_ref[pl.ds(pl.multiple_of(i * PEER_NKEYS, PEER_NKEYS), PEER_NKEYS), lc * 128:(lc + 1) * 128] = acc.astype(BF16)
        return carry

    lax.fori_loop(0, PEER_NKEYS, gate_rows, 0)


def _peer_select(qp, keys16):
    t = qp.shape[0]
    tt = 256
    n_exp = PEER_NKEYS * PEER_NKEYS
    stage = pltpu.VMEM((PEER_HEADS, tt // 128, PEER_NKEYS, 128), F32)
    return pl.pallas_call(
        _peer_select_kernel,
        grid=(t // tt,),
        in_specs=[
            pl.BlockSpec((tt, qp.shape[1]), lambda i: (i, 0)),
            pl.BlockSpec(keys16.shape, lambda i: (0, 0, 0)),
        ],
        out_specs=pl.BlockSpec((n_exp, tt), lambda i: (0, i)),
        out_shape=jax.ShapeDtypeStruct((n_exp, t), BF16),
        scratch_shapes=[stage, stage, stage, stage],
        compiler_params=_cparams(1, 48),
        name="peer_select",
    )(qp, keys16)


def _peer_dense_kernel(h_ref, u_ref, v_ref, w_ref, x_ref, mod_ref, o_ref):
    e = pl.program_id(1)

    @pl.when(e == 0)
    def _():
        o_ref[...] = jnp.zeros_like(o_ref)

    at = _dot_nt(u_ref[...], h_ref[...])
    gelu = 0.5 * at * (1.0 + lax.erf(at * float(np.sqrt(0.5))))
    wt = (w_ref[...].astype(F32) * gelu).astype(BF16)
    o_ref[...] += _dot_tn(wt, v_ref[...])

    @pl.when(e == pl.num_programs(1) - 1)
    def _():
        o_ref[...] = x_ref[...] + mod_ref[0, 5:6, :] * o_ref[...]


def _peer_dense(h2, u16, v16, wgt, x1, mod3, seq):
    t, d = h2.shape
    n_exp = u16.shape[0]
    tm = 512 if t % 512 == 0 else t
    tm = min(tm, seq)
    te = 512
    return pl.pallas_call(
        _peer_dense_kernel,
        grid=(t // tm, n_exp // te),
        in_specs=[
            pl.BlockSpec((tm, d), lambda i, e: (i, 0)),
            pl.BlockSpec((te, d), lambda i, e: (e, 0)),
            pl.BlockSpec((te, d), lambda i, e: (e, 0)),
            pl.BlockSpec((te, tm), lambda i, e: (e, i)),
            pl.BlockSpec((tm, d), lambda i, e: (i, 0), pipeline_mode=pl.Buffered(1)),
            pl.BlockSpec((1, N_MOD, d), lambda i, e: ((i * tm) // seq, 0, 0)),
        ],
        out_specs=pl.BlockSpec((tm, d), lambda i, e: (i, 0)),
        out_shape=jax.ShapeDtypeStruct((t, d), F32),
        compiler_params=_cparams(2, 56),
        name="peer_dense",
    )(h2, u16, v16, wgt, x1, mod3)


def _layer(x, c8, w_ada, b_ada, norm1_g, w_in, b_f, qn_m, kn_m, qn_f, kn_f, w_out, norm2_g,
           w_pq, sub_keys, peer_u, peer_v):
    b, s, d = x.shape
    t = b * s
    n_heads = d // HEAD_DIM
    n_moba = n_heads // 2
    n_fox = n_heads - n_moba
    d_moba = n_moba * HEAD_DIM
    d_fox = n_fox * HEAD_DIM
    d_qkv = 3 * d_moba + 3 * d_fox
    assert s % ATTN_TILE == 0 and ATTN_TILE % MOBA_BLOCK == 0 and s % 512 == 0
    assert n_fox <= FG_GROUP and s // MOBA_BLOCK <= 16 and b <= 8

    mod = _adaln(c8, w_ada, b_ada.reshape(1, -1))
    mod3 = mod[:b].reshape(b, N_MOD, d)

    w_in_t = jnp.swapaxes(w_in, 0, 1)
    w_fg_t = w_in_t[d_qkv:]
    wfg3 = jnp.zeros((128, d), F32)
    bfg3 = jnp.zeros((1, 128), F32)
    for g in range(3):
        wfg3 = wfg3.at[g * FG_GROUP:g * FG_GROUP + n_fox, :].set(w_fg_t)
        bfg3 = bfg3.at[0, g * FG_GROUP:g * FG_GROUP + n_fox].set(b_f)
    h1, fpack = _norm1(x, norm1_g.reshape(1, d), mod3, wfg3.astype(BF16), bfg3, n_fox)

    proj = _matmul(h1.reshape(t, d), w_in_t, d_qkv, F32, "in_proj", w_rows_are_outputs=True)

    slopes = 2.0 ** (-8.0 * jnp.arange(1, n_moba + 1, dtype=F32) / n_moba)
    o_moba, u16 = _moba(proj, slopes, qn_m.reshape(1, -1), kn_m.reshape(1, -1), b, s, n_moba, 0, peer_u)
    o_fox, v16 = _fox(proj, fpack, qn_f.reshape(1, -1), kn_f.reshape(1, -1), b, s, n_fox, 3 * n_moba, peer_v)

    x1 = _outproj(o_moba, o_fox, w_out, x.reshape(t, d), mod3, s).reshape(b, s, d)

    h2 = _norm2(x1, norm2_g.reshape(1, d), mod3).reshape(t, d)
    qp = _matmul(h2, w_pq, w_pq.shape[1], F32, "peer_query")
    keys16 = sub_keys.reshape(2 * PEER_HEADS, PEER_NKEYS, PEER_HALF).astype(BF16)
    wgt = _peer_select(qp, keys16)
    out = _peer_dense(h2, u16, v16, wgt, x1.reshape(t, d), mod3, s)
    return out.reshape(b, s, d)


def kernel(x, c, w_ada, b_ada, norm1_g, w_in, b_f, q_norm_moba, k_norm_moba, q_norm_fox, k_norm_fox,
           w_out, norm2_g, w_pq, peer_sub_keys, peer_u, peer_v):
    b = x.shape[0]
    c8 = jnp.zeros((8, c.shape[1]), F32).at[:b].set(c)
    for l in range(w_ada.shape[0]):
        x = _layer(x, c8, w_ada[l], b_ada[l], norm1_g[l], w_in[l], b_f[l], q_norm_moba[l], k_norm_moba[l],
                   q_norm_fox[l], k_norm_fox[l], w_out[l], norm2_g[l], w_pq[l], peer_sub_keys[l],
                   peer_u[l], peer_v[l])
    return x
```

```python
import functools

import jax
import jax.numpy as jnp
import numpy as np
from jax import lax
from jax.experimental import pallas as pl
from jax.experimental.pallas import tpu as pltpu

F32 = jnp.float32
BF16 = jnp.bfloat16

HEAD_DIM = 128
MOBA_BLOCK = 256
MOBA_BLOCK_LOG2 = 8
MOBA_TOPK = 3
PEER_HEADS = 8
PEER_NKEYS = 128
PEER_TOPK = 16
PEER_HALF = 128
N_MOD = 6
NORM_EPS = 1e-6
NEG_INF = -1e30
MASK_BIG = 2.0 ** 100
ATTN_TILE = 512
LOG2E = 1.4426950408889634
FG_GROUP = 16

MIB = 1024 * 1024


def _cparams(n_axes, vmem_mib):
    return pltpu.CompilerParams(
        dimension_semantics=("arbitrary",) * n_axes,
        vmem_limit_bytes=int(vmem_mib * MIB),
    )


def _dot(a, b):
    return jnp.dot(a, b, preferred_element_type=F32)


def _dot_nt(a, b):
    return lax.dot_general(a, b, (((1,), (1,)), ((), ())), preferred_element_type=F32)


def _dot_tn(a, b):
    return lax.dot_general(a, b, (((0,), (0,)), ((), ())), preferred_element_type=F32)


def _split3(x):
    p1 = x.astype(BF16)
    r1 = x - p1.astype(F32)
    p2 = r1.astype(BF16)
    r2 = r1 - p2.astype(F32)
    p3 = r2.astype(BF16)
    return p1, p2, p3


def _rms(x, g):
    ms = jnp.mean(x * x, axis=-1, keepdims=True)
    return x * lax.rsqrt(ms + NORM_EPS) * g


def _adaln_kernel(c_ref, w_ref, b_ref, o_ref):
    c = c_ref[...]
    ca = c * (1.0 / (1.0 + jnp.exp(-c)))
    o_ref[...] = _dot(ca.astype(BF16), w_ref[...].astype(BF16)) + b_ref[...]


def _adaln(c_pad, w, b_row):
    rows, d = c_pad.shape
    n = w.shape[1]
    tn = 512
    return pl.pallas_call(
        _adaln_kernel,
        grid=(n // tn,),
        in_specs=[
            pl.BlockSpec((rows, d), lambda j: (0, 0)),
            pl.BlockSpec((d, tn), lambda j: (0, j)),
            pl.BlockSpec((1, tn), lambda j: (0, j)),
        ],
        out_specs=pl.BlockSpec((rows, tn), lambda j: (0, j)),
        out_shape=jax.ShapeDtypeStruct((rows, n), F32),
        compiler_params=_cparams(1, 40),
        name="adaln",
    )(c_pad, w, b_row)


def _norm_mod(x_ref, g_ref, mod_ref, row0):
    xf = x_ref[0]
    y = _rms(xf, g_ref[...])
    sh = mod_ref[0, row0:row0 + 1, :]
    sc = mod_ref[0, row0 + 1:row0 + 2, :]
    return y * (1.0 + sc) + sh


def _norm1_kernel(x_ref, g_ref, mod_ref, wfg_ref, bfg_ref, h_ref, fp_ref, carry_ref, *, inv_scale, n_fox):
    s_idx = pl.program_id(1)
    hb = _norm_mod(x_ref, g_ref, mod_ref, 0).astype(BF16)
    h_ref[0] = hb
    ts = hb.shape[0]

    @pl.when(s_idx == 0)
    def _():
        carry_ref[...] = jnp.zeros_like(carry_ref)

    z = _dot_nt(hb, wfg_ref[...]) + bfg_ref[...]
    lf = jnp.minimum(z, 0.0) - jnp.log1p(jnp.exp(-jnp.abs(z)))
    r = lax.broadcasted_iota(jnp.int32, (ts, ts), 0)
    c = lax.broadcasted_iota(jnp.int32, (ts, ts), 1)
    tri = jnp.where(c <= r, 1.0, 0.0).astype(BF16)
    l1, l2, l3 = _split3(lf)
    f = (_dot(tri, l3) + _dot(tri, l2)) + _dot(tri, l1) + carry_ref[...]
    carry_ref[...] = f[ts - 1:ts, :]
    f1, f2, f3 = [p.astype(F32) for p in _split3(f * inv_scale)]
    lane = lax.broadcasted_iota(jnp.int32, f.shape, 1)
    packed = jnp.where(lane < FG_GROUP, f1,
              jnp.where(lane < 2 * FG_GROUP, f2,
               jnp.where(lane < 3 * FG_GROUP, f3,
                jnp.where(lane == 3 * FG_GROUP, 1.0, 0.0))))
    fp_ref[0] = packed.astype(BF16)


def _norm1(x, g_row, mod3, wfg3, bfg3, n_fox):
    b, s, d = x.shape
    ts = 512
    kern = functools.partial(_norm1_kernel, inv_scale=float(HEAD_DIM ** 0.5), n_fox=n_fox)
    return pl.pallas_call(
        kern,
        grid=(b, s // ts),
        in_specs=[
            pl.BlockSpec((1, ts, d), lambda i, j: (i, j, 0)),
            pl.BlockSpec((1, d), lambda i, j: (0, 0)),
            pl.BlockSpec((1, N_MOD, d), lambda i, j: (i, 0, 0)),
            pl.BlockSpec((128, d), lambda i, j: (0, 0)),
            pl.BlockSpec((1, 128), lambda i, j: (0, 0)),
        ],
        out_specs=[
            pl.BlockSpec((1, ts, d), lambda i, j: (i, j, 0)),
            pl.BlockSpec((1, ts, 128), lambda i, j: (i, j, 0)),
        ],
        out_shape=[
            jax.ShapeDtypeStruct((b, s, d), BF16),
            jax.ShapeDtypeStruct((b, s, 128), BF16),
        ],
        scratch_shapes=[pltpu.VMEM((1, 128), F32)],
        compiler_params=_cparams(2, 48),
        name="norm1_fgate",
    )(x, g_row, mod3, wfg3, bfg3)


def _norm2_kernel(x_ref, g_ref, mod_ref, h_ref):
    h_ref[0] = _norm_mod(x_ref, g_ref, mod_ref, 3).astype(BF16)


def _norm2(x, g_row, mod3):
    b, s, d = x.shape
    ts = 512
    return pl.pallas_call(
        _norm2_kernel,
        grid=(b, s // ts),
        in_specs=[
            pl.BlockSpec((1, ts, d), lambda i, j: (i, j, 0)),
            pl.BlockSpec((1, d), lambda i, j: (0, 0)),
            pl.BlockSpec((1, N_MOD, d), lambda i, j: (i, 0, 0)),
        ],
        out_specs=pl.BlockSpec((1, ts, d), lambda i, j: (i, j, 0)),
        out_shape=jax.ShapeDtypeStruct((b, s, d), BF16),
        compiler_params=_cparams(2, 48),
        name="norm2",
    )(x, g_row, mod3)


MM_TM = 1024
MM_TN = 512


def _cast_weight(w_ref, wb_ref):
    @pl.when(pl.program_id(1) == 0)
    def _():
        wb_ref[...] = w_ref[...].astype(BF16)


def _mm_kernel(a_ref, w_ref, o_ref, wb_ref, *, w_rows_are_outputs):
    _cast_weight(w_ref, wb_ref)
    dot = _dot_nt if w_rows_are_outputs else _dot
    o_ref[...] = dot(a_ref[...], wb_ref[...]).astype(o_ref.dtype)


def _matmul(a, w, n, out_dtype, name, w_rows_are_outputs=False):
    m, k = a.shape
    tm = MM_TM if m % MM_TM == 0 else m
    tn = MM_TN
    assert n % tn == 0
    if w_rows_are_outputs:
        w_block, w_spec = (tn, k), pl.BlockSpec((tn, k), lambda j, i: (j, 0))
    else:
        w_block, w_spec = (k, tn), pl.BlockSpec((k, tn), lambda j, i: (0, j))
    return pl.pallas_call(
        functools.partial(_mm_kernel, w_rows_are_outputs=w_rows_are_outputs),
        grid=(n // tn, m // tm),
        in_specs=[pl.BlockSpec((tm, k), lambda j, i: (i, 0)), w_spec],
        out_specs=pl.BlockSpec((tm, tn), lambda j, i: (i, j)),
        out_shape=jax.ShapeDtypeStruct((m, n), out_dtype),
        scratch_shapes=[pltpu.VMEM(w_block, BF16)],
        compiler_params=_cparams(2, 48),
        name=name,
    )(a, w)


def _outproj_kernel(oa_ref, ob_ref, w_ref, x_ref, mod_ref, o_ref, wb_ref):
    _cast_weight(w_ref, wb_ref)
    ka = oa_ref.shape[1]
    acc = _dot(oa_ref[...], wb_ref[0:ka, :]) + _dot(ob_ref[...], wb_ref[ka:, :])
    gate = mod_ref[0, 2:3, :]
    o_ref[...] = x_ref[...] + gate * acc


def _outproj(oa, ob, w, x2d, mod3, seq):
    m, ka = oa.shape
    kb = ob.shape[1]
    k, n = w.shape
    assert ka + kb == k
    tm = min(MM_TM if m % MM_TM == 0 else m, seq)
    tn = MM_TN
    return pl.pallas_call(
        _outproj_kernel,
        grid=(n // tn, m // tm),
        in_specs=[
            pl.BlockSpec((tm, ka), lambda j, i: (i, 0)),
            pl.BlockSpec((tm, kb), lambda j, i: (i, 0)),
            pl.BlockSpec((k, tn), lambda j, i: (0, j)),
            pl.BlockSpec((tm, tn), lambda j, i: (i, j)),
            pl.BlockSpec((1, N_MOD, tn), lambda j, i: ((i * tm) // seq, 0, j)),
        ],
        out_specs=pl.BlockSpec((tm, tn), lambda j, i: (i, j)),
        out_shape=jax.ShapeDtypeStruct((m, n), F32),
        scratch_shapes=[pltpu.VMEM((k, tn), BF16)],
        compiler_params=_cparams(2, 52),
        name="out_proj_residual",
    )(oa, ob, w, x2d, mod3)


def _attn_rows(qa_ref, ka_ref, vt_ref, o_ref, tab_ref):
    seq = qa_ref.shape[0]
    t = ATTN_TILE
    c_log2 = float(HEAD_DIM ** -0.5 * LOG2E)
    row = lax.broadcasted_iota(jnp.int32, (t, t), 0)
    col = lax.broadcasted_iota(jnp.int32, (t, t), 1)
    causal = row <= col
    for qp in range(seq // t):
        kv = (qp + 1) * t
        qa = qa_ref[qp * t:(qp + 1) * t, :]
        st = _dot_nt(ka_ref[0:kv, :], qa) * c_log2
        if tab_ref is not None:
            off = tab_ref.shape[0] - kv
            st = st - tab_ref[off:off + kv, :]
        tail = jnp.where(causal, st[kv - t:kv, :], NEG_INF)
        m = jnp.max(tail, axis=0, keepdims=True)
        if qp > 0:
            head = st[0:kv - t, :]
            m = jnp.maximum(m, jnp.max(head, axis=0, keepdims=True))
            p_head = jnp.exp2(head - m)
        p_tail = jnp.exp2(tail - m)
        l = jnp.sum(p_tail, axis=0, keepdims=True)
        acc = _dot(vt_ref[:, kv - t:kv], p_tail.astype(BF16))
        if qp > 0:
            l = l + jnp.sum(p_head, axis=0, keepdims=True)
            acc = acc + _dot(vt_ref[:, 0:kv - t], p_head.astype(BF16))
        ot = acc * (1.0 / l)
        o_ref[qp * t:(qp + 1) * t, :] = ot.T.astype(o_ref.dtype)


def _moba_kernel(slopes_ref, q_ref, k_ref, v_ref, gq_ref, gk_ref, wsrc_ref, o_ref, wdst_ref,
                 qa_ref, ka_ref, vt_ref, tab_ref):
    h = pl.program_id(1)
    wdst_ref[...] = wsrc_ref[...].astype(BF16)
    seq = q_ref.shape[0]
    t = ATTN_TILE
    nb = seq // MOBA_BLOCK
    qn = _rms(q_ref[...], gq_ref[...])
    kn = _rms(k_ref[...], gk_ref[...])
    qb16 = qn.astype(BF16)
    kb16 = kn.astype(BF16)

    rid = lax.broadcasted_iota(jnp.int32, (16, HEAD_DIM), 0)
    kmean = jnp.zeros((16, HEAD_DIM), F32)
    for n in range(nb):
        blk = jnp.sum(kn[n * MOBA_BLOCK:(n + 1) * MOBA_BLOCK, :], axis=0, keepdims=True) / float(MOBA_BLOCK)
        kmean = jnp.where(rid == n, blk, kmean)
    gt = _dot_nt(kmean.astype(BF16), qb16)

    bidx = lax.broadcasted_iota(jnp.int32, (16, seq), 0)
    qblk = lax.shift_right_logical(lax.broadcasted_iota(jnp.int32, (16, seq), 1), MOBA_BLOCK_LOG2)
    rank = jnp.zeros((16, seq), F32)
    for n in range(nb):
        gn = gt[n:n + 1, :]
        beats = jnp.where(gn > gt, 1.0, jnp.where(gn == gt, jnp.where(bidx > n, 1.0, 0.0), 0.0))
        rank = rank + jnp.where(qblk > n, beats, 0.0)
    keep = jnp.where(bidx == qblk, 1.0,
                     jnp.where(bidx < qblk, jnp.where(rank < float(MOBA_TOPK), 1.0, 0.0), 0.0))
    bias_t = jnp.where(keep > 0.5, 0.0, -MASK_BIG)
    bias_pad = jnp.concatenate([bias_t, jnp.zeros((HEAD_DIM - 16, seq), F32)], axis=0)
    bias_nat = bias_pad.T

    qa_ref[:, 0:HEAD_DIM] = qb16
    qa_ref[:, HEAD_DIM:2 * HEAD_DIM] = bias_nat.astype(BF16)
    krow = lax.shift_right_logical(lax.broadcasted_iota(jnp.int32, (seq, HEAD_DIM), 0), MOBA_BLOCK_LOG2)
    klane = lax.broadcasted_iota(jnp.int32, (seq, HEAD_DIM), 1)
    ka_ref[:, 0:HEAD_DIM] = kb16
    ka_ref[:, HEAD_DIM:2 * HEAD_DIM] = jnp.where(klane == krow, 1.0, 0.0).astype(BF16)
    vt_ref[...] = v_ref[...].T.astype(BF16)

    slope2 = slopes_ref[h] * LOG2E
    jrow = lax.broadcasted_iota(jnp.int32, (seq, t), 0)
    qcol = lax.broadcasted_iota(jnp.int32, (seq, t), 1)
    tab_ref[...] = slope2 * (qcol - jrow + (seq - t)).astype(F32)

    _attn_rows(qa_ref, ka_ref, vt_ref, o_ref, tab_ref)


def _fox_kernel(q_ref, k_ref, v_ref, gq_ref, gk_ref, fp_ref, wsrc_ref, o_ref, wdst_ref,
                qa_ref, ka_ref, vt_ref):
    h = pl.program_id(1)
    wdst_ref[...] = wsrc_ref[...].astype(BF16)
    qn = _rms(q_ref[...], gq_ref[...])
    kn = _rms(k_ref[...], gk_ref[...])
    packed = fp_ref[0]
    r = lax.broadcasted_iota(jnp.int32, (128, 128), 0)
    c = lax.broadcasted_iota(jnp.int32, (128, 128), 1)
    ones_row = r == 3 * FG_GROUP
    part = jnp.where(r == h, 0, jnp.where(r == FG_GROUP + h, 1, jnp.where(r == 2 * FG_GROUP + h, 2, -1)))
    pq = jnp.where(part == c, 1.0, jnp.where(ones_row & (c >= 3) & (c < 6), 1.0, 0.0)).astype(BF16)
    pk = jnp.where((part + 3 == c) & (part >= 0), -1.0, jnp.where(ones_row & (c < 3), 1.0, 0.0)).astype(BF16)
    qa_ref[:, 0:HEAD_DIM] = qn.astype(BF16)
    qa_ref[:, HEAD_DIM:2 * HEAD_DIM] = _dot(packed, pq).astype(BF16)
    ka_ref[:, 0:HEAD_DIM] = kn.astype(BF16)
    ka_ref[:, HEAD_DIM:2 * HEAD_DIM] = _dot(packed, pk).astype(BF16)
    vt_ref[...] = v_ref[...].T.astype(BF16)
    _attn_rows(qa_ref, ka_ref, vt_ref, o_ref, None)


def _attn_scratch(seq):
    return [
        pltpu.VMEM((seq, 2 * HEAD_DIM), BF16),
        pltpu.VMEM((seq, 2 * HEAD_DIM), BF16),
        pltpu.VMEM((HEAD_DIM, seq), BF16),
    ]


def _cast_slab_specs(w, n_steps, step_of):
    rows, cols = w.shape
    slab = rows // n_steps
    assert slab * n_steps == rows and slab % 16 == 0
    spec = pl.BlockSpec((slab, cols), lambda *g: (step_of(*g), 0))
    return spec, spec, jax.ShapeDtypeStruct((rows, cols), BF16)


def _moba(proj, slopes, gq, gk, batch, seq, n_heads, col0, w_f32):
    hspec = lambda off: pl.BlockSpec((seq, HEAD_DIM), lambda b, h: (b, col0 + off + h))
    w_in_spec, w_out_spec, w_shape = _cast_slab_specs(w_f32, n_heads * batch, lambda b, h: b * n_heads + h)
    return pl.pallas_call(
        _moba_kernel,
        grid=(batch, n_heads),
        in_specs=[
            pl.BlockSpec(memory_space=pltpu.SMEM),
            hspec(0), hspec(n_heads), hspec(2 * n_heads),
            pl.BlockSpec((1, HEAD_DIM), lambda b, h: (0, 0)),
            pl.BlockSpec((1, HEAD_DIM), lambda b, h: (0, 0)),
            w_in_spec,
        ],
        out_specs=[pl.BlockSpec((seq, HEAD_DIM), lambda b, h: (b, h)), w_out_spec],
        out_shape=[jax.ShapeDtypeStruct((batch * seq, n_heads * HEAD_DIM), BF16), w_shape],
        scratch_shapes=_attn_scratch(seq) + [pltpu.VMEM((seq, ATTN_TILE), F32)],
        compiler_params=_cparams(2, 56),
        name="moba_attention",
    )(slopes, proj, proj, proj, gq, gk, w_f32)


def _fox(proj, fpack, gq, gk, batch, seq, n_heads, col0, w_f32):
    hspec = lambda off: pl.BlockSpec((seq, HEAD_DIM), lambda b, h: (b, col0 + off + h))
    w_in_spec, w_out_spec, w_shape = _cast_slab_specs(w_f32, n_heads * batch, lambda b, h: b * n_heads + h)
    return pl.pallas_call(
        _fox_kernel,
        grid=(batch, n_heads),
        in_specs=[
            hspec(0), hspec(n_heads), hspec(2 * n_heads),
            pl.BlockSpec((1, HEAD_DIM), lambda b, h: (0, 0)),
            pl.BlockSpec((1, HEAD_DIM), lambda b, h: (0, 0)),
            pl.BlockSpec((1, seq, 128), lambda b, h: (b, 0, 0)),
            w_in_spec,
        ],
        out_specs=[pl.BlockSpec((seq, HEAD_DIM), lambda b, h: (b, h)), w_out_spec],
        out_shape=[jax.ShapeDtypeStruct((batch * seq, n_heads * HEAD_DIM), BF16), w_shape],
        scratch_shapes=_attn_scratch(seq),
        compiler_params=_cparams(2, 56),
        name="fox_attention",
    )(proj, proj, proj, gq, gk, fpack, w_f32)


def _top16_sorted(st, want_rank=False):
    tt = st.shape[1]
    rid = lax.broadcasted_iota(jnp.int32, (PEER_TOPK, tt), 0)
    sv = jnp.zeros((PEER_TOPK, tt), F32)
    rank = jnp.full(st.shape, float(PEER_TOPK), F32) if want_rank else None
    work = st
    for r in range(PEER_TOPK):
        mx = jnp.max(work, axis=0, keepdims=True)
        sv = jnp.where(rid == r, mx, sv)
        hit = work == mx
        if want_rank:
            rank = jnp.where(hit, float(r), rank)
        if r + 1 < PEER_TOPK:
            work = jnp.where(hit, -jnp.inf, work)
    return (sv, rank) if want_rank else sv


def _peer_select_kernel(qp_ref, keys_ref, w_ref, r2_s, e2_s, k1_s, c1_s):
    tt = qp_ref.shape[0]
    n_lc = tt // 128
    b8 = lax.broadcasted_iota(jnp.int32, (8, tt), 0)
    inf = jnp.inf
    for h in range(PEER_HEADS):
        q1 = qp_ref[:, (2 * h) * PEER_HALF:(2 * h + 1) * PEER_HALF].astype(BF16)
        q2 = qp_ref[:, (2 * h + 1) * PEER_HALF:(2 * h + 2) * PEER_HALF].astype(BF16)
        s1 = _dot_nt(keys_ref[2 * h], q1)
        s2 = _dot_nt(keys_ref[2 * h + 1], q2)
        sv0 = _top16_sorted(s1)
        sv1, rank2 = _top16_sorted(s2, want_rank=True)
        sv1a, sv1b = sv1[0:8, :], sv1[8:16, :]
        row = lambda a: sv0[a:a + 1, :]
        cands = [(row(0) + sv1a, sv1a), (row(0) + sv1b, sv1b), (row(1) + sv1a, sv1a)]
        for a, nbv in ((2, 5), (3, 4), (4, 3), (5, 2), (6, 2), (7, 2)):
            cands.append((jnp.where(b8 < nbv, row(a) + sv1a, -inf), sv1a))
        top1 = sv1[0:1, :]
        hsum = sv0[8:16, :] + top1
        work = [g for g, _ in cands] + [hsum]
        tau = None
        for r in range(PEER_TOPK):
            mx = functools.reduce(jnp.maximum, work)
            tau = jnp.max(mx, axis=0, keepdims=True)
            if r + 1 < PEER_TOPK:
                work = [jnp.where(g == tau, -inf, g) for g in work]
        top = row(0) + top1
        zsum = jnp.zeros((1, tt), F32)
        for g in [g for g, _ in cands] + [hsum]:
            zsum = zsum + jnp.sum(jnp.where(g >= tau, jnp.exp(g - top), 0.0), axis=0, keepdims=True)
        inv_z = 1.0 / zsum
        cnt = lambda g: jnp.sum(jnp.where(g >= tau, 1.0, 0.0), axis=0, keepdims=True)
        k_rows = [cnt(cands[0][0]) + cnt(cands[1][0])] + [cnt(g) for g, _ in cands[2:]]
        k_hi = jnp.where(hsum >= tau, 1.0, 0.0)
        k1 = jnp.zeros(s1.shape, F32)
        for a in range(PEER_TOPK):
            ka = k_rows[a] if a < 8 else k_hi[a - 8:a - 7, :]
            k1 = jnp.where(s1 == row(a), ka, k1)
        e2 = jnp.exp(s2 - top1)
        c1 = jnp.exp(s1 - row(0)) * inv_z
        for lc in range(n_lc):
            cols = slice(lc * 128, (lc + 1) * 128)
            r2_s[h, lc] = rank2[:, cols].astype(BF16)
            e2_s[h, lc] = e2[:, cols].astype(BF16)
            k1_s[h, lc] = k1[:, cols]
            c1_s[h, lc] = c1[:, cols]

    def gate_rows(i, carry):
        for lc in range(n_lc):
            acc = None
            for h in range(PEER_HEADS):
                k1 = jnp.broadcast_to(k1_s[h, lc, pl.ds(i, 1), :], (16, 128)).astype(BF16)
                c1 = jnp.broadcast_to(c1_s[h, lc, pl.ds(i, 1), :], (16, 128)).astype(BF16)
                k1 = jnp.tile(k1, (PEER_NKEYS // 16, 1))
                c1 = jnp.tile(c1, (PEER_NKEYS // 16, 1))
                w = jnp.where(r2_s[h, lc] < k1, e2_s[h, lc] * c1, jnp.zeros_like(c1))
                acc = w if acc is None else acc + w
            w_ref[pl.ds(pl.multiple_of(i * PEER_NKEYS, PEER_NKEYS), PEER_NKEYS), lc * 128:(lc + 1) * 128] = acc
        return carry

    lax.fori_loop(0, PEER_NKEYS, gate_rows, 0)


def _peer_select(qp, keys16):
    t = qp.shape[0]
    tt = 256
    n_exp = PEER_NKEYS * PEER_NKEYS
    stage = pltpu.VMEM((PEER_HEADS, tt // 128, PEER_NKEYS, 128), F32)
    stage16 = pltpu.VMEM((PEER_HEADS, tt // 128, PEER_NKEYS, 128), BF16)
    return pl.pallas_call(
        _peer_select_kernel,
        grid=(t // tt,),
        in_specs=[
            pl.BlockSpec((tt, qp.shape[1]), lambda i: (i, 0)),
            pl.BlockSpec(keys16.shape, lambda i: (0, 0, 0)),
        ],
        out_specs=pl.BlockSpec((n_exp, tt), lambda i: (0, i)),
        out_shape=jax.ShapeDtypeStruct((n_exp, t), BF16),
        scratch_shapes=[stage16, stage16, stage, stage],
        compiler_params=_cparams(1, 48),
        name="peer_select",
    )(qp, keys16)


def _peer_dense_kernel(h_ref, u_ref, v_ref, w_ref, x_ref, mod_ref, o_ref):
    e = pl.program_id(1)

    @pl.when(e == 0)
    def _():
        o_ref[...] = jnp.zeros_like(o_ref)

    at = _dot_nt(u_ref[...], h_ref[...])
    gelu = 0.5 * at * (1.0 + lax.erf(at * float(np.sqrt(0.5))))
    wt = (w_ref[...].astype(F32) * gelu).astype(BF16)
    o_ref[...] += _dot_tn(wt, v_ref[...])

    @pl.when(e == pl.num_programs(1) - 1)
    def _():
        o_ref[...] = x_ref[...] + mod_ref[0, 5:6, :] * o_ref[...]


def _peer_dense(h2, u16, v16, wgt, x1, mod3, seq):
    t, d = h2.shape
    n_exp = u16.shape[0]
    tm = 512 if t % 512 == 0 else t
    tm = min(tm, seq)
    te = 512
    return pl.pallas_call(
        _peer_dense_kernel,
        grid=(t // tm, n_exp // te),
        in_specs=[
            pl.BlockSpec((tm, d), lambda i, e: (i, 0)),
            pl.BlockSpec((te, d), lambda i, e: (e, 0)),
            pl.BlockSpec((te, d), lambda i, e: (e, 0)),
            pl.BlockSpec((te, tm), lambda i, e: (e, i)),
            pl.BlockSpec((tm, d), lambda i, e: (i, 0), pipeline_mode=pl.Buffered(1)),
            pl.BlockSpec((1, N_MOD, d), lambda i, e: ((i * tm) // seq, 0, 0)),
        ],
        out_specs=pl.BlockSpec((tm, d), lambda i, e: (i, 0)),
        out_shape=jax.ShapeDtypeStruct((t, d), F32),
        compiler_params=_cparams(2, 56),
        name="peer_dense",
    )(h2, u16, v16, wgt, x1, mod3)


def _layer(x, c8, w_ada, b_ada, norm1_g, w_in, b_f, qn_m, kn_m, qn_f, kn_f, w_out, norm2_g,
           w_pq, sub_keys, peer_u, peer_v):
    b, s, d = x.shape
    t = b * s
    n_heads = d // HEAD_DIM
    n_moba = n_heads // 2
    n_fox = n_heads - n_moba
    d_moba = n_moba * HEAD_DIM
    d_fox = n_fox * HEAD_DIM
    d_qkv = 3 * d_moba + 3 * d_fox
    assert s % ATTN_TILE == 0 and ATTN_TILE % MOBA_BLOCK == 0 and s % 512 == 0
    assert n_fox <= FG_GROUP and s // MOBA_BLOCK <= 16 and b <= 8

    mod = _adaln(c8, w_ada, b_ada.reshape(1, -1))
    mod3 = mod[:b].reshape(b, N_MOD, d)

    w_in_t = jnp.swapaxes(w_in, 0, 1)
    w_fg_t = w_in_t[d_qkv:]
    wfg3 = jnp.zeros((128, d), F32)
    bfg3 = jnp.zeros((1, 128), F32)
    for g in range(3):
        wfg3 = wfg3.at[g * FG_GROUP:g * FG_GROUP + n_fox, :].set(w_fg_t)
        bfg3 = bfg3.at[0, g * FG_GROUP:g * FG_GROUP + n_fox].set(b_f)
    h1, fpack = _norm1(x, norm1_g.reshape(1, d), mod3, wfg3.astype(BF16), bfg3, n_fox)

    proj = _matmul(h1.reshape(t, d), w_in_t, d_qkv, F32, "in_proj", w_rows_are_outputs=True)

    slopes = 2.0 ** (-8.0 * jnp.arange(1, n_moba + 1, dtype=F32) / n_moba)
    o_moba, u16 = _moba(proj, slopes, qn_m.reshape(1, -1), kn_m.reshape(1, -1), b, s, n_moba, 0, peer_u)
    o_fox, v16 = _fox(proj, fpack, qn_f.reshape(1, -1), kn_f.reshape(1, -1), b, s, n_fox, 3 * n_moba, peer_v)

    x1 = _outproj(o_moba, o_fox, w_out, x.reshape(t, d), mod3, s).reshape(b, s, d)

    h2 = _norm2(x1, norm2_g.reshape(1, d), mod3).reshape(t, d)
    qp = _matmul(h2, w_pq, w_pq.shape[1], F32, "peer_query")
    keys16 = sub_keys.reshape(2 * PEER_HEADS, PEER_NKEYS, PEER_HALF).astype(BF16)
    wgt = _peer_select(qp, keys16)
    out = _peer_dense(h2, u16, v16, wgt, x1.reshape(t, d), mod3, s)
    return out.reshape(b, s, d)


def kernel(x, c, w_ada, b_ada, norm1_g, w_in, b_f, q_norm_moba, k_norm_moba, q_norm_fox, k_norm_fox,
           w_out, norm2_g, w_pq, peer_sub_keys, peer_u, peer_v):
    b = x.shape[0]
    c8 = jnp.zeros((8, c.shape[1]), F32).at[:b].set(c)
    for l in range(w_ada.shape[0]):
        x = _layer(x, c8, w_ada[l], b_ada[l], norm1_g[l], w_in[l], b_f[l], q_norm_moba[l], k_norm_moba[l],
                   q_norm_fox[l], k_norm_fox[l], w_out[l], norm2_g[l], w_pq[l], peer_sub_keys[l],
                   peer_u[l], peer_v[l])
    return x
```

```python
import functools

import jax
import jax.numpy as jnp
import numpy as np
from jax import lax
from jax.experimental import pallas as pl
from jax.experimental.pallas import tpu as pltpu

F32 = jnp.float32
BF16 = jnp.bfloat16

HEAD_DIM = 128
MOBA_BLOCK = 256
MOBA_BLOCK_LOG2 = 8
MOBA_TOPK = 3
PEER_HEADS = 8
PEER_NKEYS = 128
PEER_TOPK = 16
PEER_HALF = 128
N_MOD = 6
NORM_EPS = 1e-6
NEG_INF = -1e30
MASK_BIG = 2.0 ** 100
ATTN_TILE = 512
LOG2E = 1.4426950408889634
FG_GROUP = 16
VT_PAD = 16

MIB = 1024 * 1024


def _cparams(n_axes, vmem_mib):
    return pltpu.CompilerParams(
        dimension_semantics=("arbitrary",) * n_axes,
        vmem_limit_bytes=int(vmem_mib * MIB),
    )


def _dot(a, b):
    return jnp.dot(a, b, preferred_element_type=F32)


def _dot_nt(a, b):
    return lax.dot_general(a, b, (((1,), (1,)), ((), ())), preferred_element_type=F32)


def _dot_tn(a, b):
    return lax.dot_general(a, b, (((0,), (0,)), ((), ())), preferred_element_type=F32)


def _split3(x):
    p1 = x.astype(BF16)
    r1 = x - p1.astype(F32)
    p2 = r1.astype(BF16)
    r2 = r1 - p2.astype(F32)
    p3 = r2.astype(BF16)
    return p1, p2, p3


def _rms(x, g):
    ms = jnp.mean(x * x, axis=-1, keepdims=True)
    return x * lax.rsqrt(ms + NORM_EPS) * g


def _adaln_kernel(c_ref, w_ref, b_ref, o_ref):
    c = c_ref[...]
    ca = c * (1.0 / (1.0 + jnp.exp(-c)))
    o_ref[...] = _dot(ca.astype(BF16), w_ref[...].astype(BF16)) + b_ref[...]


def _adaln(c_pad, w, b_row):
    rows, d = c_pad.shape
    n = w.shape[1]
    tn = 512
    return pl.pallas_call(
        _adaln_kernel,
        grid=(n // tn,),
        in_specs=[
            pl.BlockSpec((rows, d), lambda j: (0, 0)),
            pl.BlockSpec((d, tn), lambda j: (0, j)),
            pl.BlockSpec((1, tn), lambda j: (0, j)),
        ],
        out_specs=pl.BlockSpec((rows, tn), lambda j: (0, j)),
        out_shape=jax.ShapeDtypeStruct((rows, n), F32),
        compiler_params=_cparams(1, 40),
        name="adaln",
    )(c_pad, w, b_row)


def _norm_mod(x_ref, g_ref, mod_ref, row0):
    xf = x_ref[0]
    y = _rms(xf, g_ref[...])
    sh = mod_ref[0, row0:row0 + 1, :]
    sc = mod_ref[0, row0 + 1:row0 + 2, :]
    return y * (1.0 + sc) + sh


def _norm1_kernel(x_ref, g_ref, mod_ref, wfg_ref, bfg_ref, h_ref, fp_ref, carry_ref, *, inv_scale, n_fox):
    s_idx = pl.program_id(1)
    hb = _norm_mod(x_ref, g_ref, mod_ref, 0).astype(BF16)
    h_ref[0] = hb
    ts = hb.shape[0]

    @pl.when(s_idx == 0)
    def _():
        carry_ref[...] = jnp.zeros_like(carry_ref)

    z = _dot_nt(hb, wfg_ref[...]) + bfg_ref[...]
    lf = jnp.minimum(z, 0.0) - jnp.log1p(jnp.exp(-jnp.abs(z)))
    r = lax.broadcasted_iota(jnp.int32, (ts, ts), 0)
    c = lax.broadcasted_iota(jnp.int32, (ts, ts), 1)
    tri = jnp.where(c <= r, 1.0, 0.0).astype(BF16)
    l1, l2, l3 = _split3(lf)
    f = (_dot(tri, l3) + _dot(tri, l2)) + _dot(tri, l1) + carry_ref[...]
    carry_ref[...] = f[ts - 1:ts, :]
    f1, f2, f3 = [p.astype(F32) for p in _split3(f * inv_scale)]
    lane = lax.broadcasted_iota(jnp.int32, f.shape, 1)
    packed = jnp.where(lane < FG_GROUP, f1,
              jnp.where(lane < 2 * FG_GROUP, f2,
               jnp.where(lane < 3 * FG_GROUP, f3,
                jnp.where(lane == 3 * FG_GROUP, 1.0, 0.0))))
    fp_ref[0] = packed.astype(BF16)


def _norm1(x, g_row, mod3, wfg3, bfg3, n_fox):
    b, s, d = x.shape
    ts = 512
    kern = functools.partial(_norm1_kernel, inv_scale=float(HEAD_DIM ** 0.5), n_fox=n_fox)
    return pl.pallas_call(
        kern,
        grid=(b, s // ts),
        in_specs=[
            pl.BlockSpec((1, ts, d), lambda i, j: (i, j, 0)),
            pl.BlockSpec((1, d), lambda i, j: (0, 0)),
            pl.BlockSpec((1, N_MOD, d), lambda i, j: (i, 0, 0)),
            pl.BlockSpec((128, d), lambda i, j: (0, 0)),
            pl.BlockSpec((1, 128), lambda i, j: (0, 0)),
        ],
        out_specs=[
            pl.BlockSpec((1, ts, d), lambda i, j: (i, j, 0)),
            pl.BlockSpec((1, ts, 128), lambda i, j: (i, j, 0)),
        ],
        out_shape=[
            jax.ShapeDtypeStruct((b, s, d), BF16),
            jax.ShapeDtypeStruct((b, s, 128), BF16),
        ],
        scratch_shapes=[pltpu.VMEM((1, 128), F32)],
        compiler_params=_cparams(2, 48),
        name="norm1_fgate",
    )(x, g_row, mod3, wfg3, bfg3)


def _norm2_kernel(x_ref, g_ref, mod_ref, h_ref):
    h_ref[0] = _norm_mod(x_ref, g_ref, mod_ref, 3).astype(BF16)


def _norm2(x, g_row, mod3):
    b, s, d = x.shape
    ts = 512
    return pl.pallas_call(
        _norm2_kernel,
        grid=(b, s // ts),
        in_specs=[
            pl.BlockSpec((1, ts, d), lambda i, j: (i, j, 0)),
            pl.BlockSpec((1, d), lambda i, j: (0, 0)),
            pl.BlockSpec((1, N_MOD, d), lambda i, j: (i, 0, 0)),
        ],
        out_specs=pl.BlockSpec((1, ts, d), lambda i, j: (i, j, 0)),
        out_shape=jax.ShapeDtypeStruct((b, s, d), BF16),
        compiler_params=_cparams(2, 48),
        name="norm2",
    )(x, g_row, mod3)


MM_TM = 1024
MM_TN = 512


def _cast_weight(w_ref, wb_ref):
    @pl.when(pl.program_id(1) == 0)
    def _():
        wb_ref[...] = w_ref[...].astype(BF16)


def _mm_kernel(a_ref, w_ref, o_ref, wb_ref, *, w_rows_are_outputs):
    _cast_weight(w_ref, wb_ref)
    dot = _dot_nt if w_rows_are_outputs else _dot
    o_ref[...] = dot(a_ref[...], wb_ref[...]).astype(o_ref.dtype)


def _matmul(a, w, n, out_dtype, name, w_rows_are_outputs=False):
    m, k = a.shape
    tm = MM_TM if m % MM_TM == 0 else m
    tn = MM_TN
    assert n % tn == 0
    if w_rows_are_outputs:
        w_block, w_spec = (tn, k), pl.BlockSpec((tn, k), lambda j, i: (j, 0))
    else:
        w_block, w_spec = (k, tn), pl.BlockSpec((k, tn), lambda j, i: (0, j))
    return pl.pallas_call(
        functools.partial(_mm_kernel, w_rows_are_outputs=w_rows_are_outputs),
        grid=(n // tn, m // tm),
        in_specs=[pl.BlockSpec((tm, k), lambda j, i: (i, 0)), w_spec],
        out_specs=pl.BlockSpec((tm, tn), lambda j, i: (i, j)),
        out_shape=jax.ShapeDtypeStruct((m, n), out_dtype),
        scratch_shapes=[pltpu.VMEM(w_block, BF16)],
        compiler_params=_cparams(2, 48),
        name=name,
    )(a, w)


def _outproj_kernel(oa_ref, ob_ref, w_ref, x_ref, mod_ref, o_ref, wb_ref):
    _cast_weight(w_ref, wb_ref)
    ka = oa_ref.shape[1]
    acc = _dot(oa_ref[...], wb_ref[0:ka, :]) + _dot(ob_ref[...], wb_ref[ka:, :])
    gate = mod_ref[0, 2:3, :]
    o_ref[...] = x_ref[...] + gate * acc


def _outproj(oa, ob, w, x2d, mod3, seq):
    m, ka = oa.shape
    kb = ob.shape[1]
    k, n = w.shape
    assert ka + kb == k
    tm = min(MM_TM if m % MM_TM == 0 else m, seq)
    tn = MM_TN
    return pl.pallas_call(
        _outproj_kernel,
        grid=(n // tn, m // tm),
        in_specs=[
            pl.BlockSpec((tm, ka), lambda j, i: (i, 0)),
            pl.BlockSpec((tm, kb), lambda j, i: (i, 0)),
            pl.BlockSpec((k, tn), lambda j, i: (0, j)),
            pl.BlockSpec((tm, tn), lambda j, i: (i, j)),
            pl.BlockSpec((1, N_MOD, tn), lambda j, i: ((i * tm) // seq, 0, j)),
        ],
        out_specs=pl.BlockSpec((tm, tn), lambda j, i: (i, j)),
        out_shape=jax.ShapeDtypeStruct((m, n), F32),
        scratch_shapes=[pltpu.VMEM((k, tn), BF16)],
        compiler_params=_cparams(2, 52),
        name="out_proj_residual",
    )(oa, ob, w, x2d, mod3)


def _store_vt(vt_ref, v):
    seq = v.shape[0]
    vt_ref[0:HEAD_DIM, :] = v.T.astype(BF16)
    rid = lax.broadcasted_iota(jnp.int32, (VT_PAD, seq), 0)
    vt_ref[HEAD_DIM:HEAD_DIM + VT_PAD, :] = jnp.where(rid == 0, 1.0, 0.0).astype(BF16)


def _attn_rows(qa_ref, ka_ref, vt_ref, o_ref):
    seq = qa_ref.shape[0]
    t = ATTN_TILE
    c_log2 = float(HEAD_DIM ** -0.5 * LOG2E)
    row = lax.broadcasted_iota(jnp.int32, (t, t), 0)
    col = lax.broadcasted_iota(jnp.int32, (t, t), 1)
    causal = row <= col
    for qp in range(seq // t):
        kv = (qp + 1) * t
        qa = qa_ref[qp * t:(qp + 1) * t, :]
        st = _dot_nt(ka_ref[0:kv, :], qa) * c_log2
        tail = jnp.where(causal, st[kv - t:kv, :], NEG_INF)
        m = jnp.max(tail, axis=0, keepdims=True)
        if qp > 0:
            head = st[0:kv - t, :]
            m = jnp.maximum(m, jnp.max(head, axis=0, keepdims=True))
            p_head = jnp.exp2(head - m)
        p_tail = jnp.exp2(tail - m)
        acc = _dot(vt_ref[:, kv - t:kv], p_tail.astype(BF16))
        if qp > 0:
            acc = acc + _dot(vt_ref[:, 0:kv - t], p_head.astype(BF16))
        ot = acc[0:HEAD_DIM, :] * (1.0 / acc[HEAD_DIM:HEAD_DIM + 1, :])
        o_ref[qp * t:(qp + 1) * t, :] = ot.T.astype(o_ref.dtype)


def _moba_kernel(slopes_ref, q_ref, k_ref, v_ref, gq_ref, gk_ref, wsrc_ref, o_ref, wdst_ref,
                 qa_ref, ka_ref, vt_ref):
    h = pl.program_id(1)
    wdst_ref[...] = wsrc_ref[...].astype(BF16)
    seq = q_ref.shape[0]
    t = ATTN_TILE
    nb = seq // MOBA_BLOCK
    qn = _rms(q_ref[...], gq_ref[...])
    kn = _rms(k_ref[...], gk_ref[...])
    qb16 = qn.astype(BF16)
    kb16 = kn.astype(BF16)

    rid = lax.broadcasted_iota(jnp.int32, (16, HEAD_DIM), 0)
    kmean = jnp.zeros((16, HEAD_DIM), F32)
    for n in range(nb):
        blk = jnp.sum(kn[n * MOBA_BLOCK:(n + 1) * MOBA_BLOCK, :], axis=0, keepdims=True) / float(MOBA_BLOCK)
        kmean = jnp.where(rid == n, blk, kmean)
    gt = _dot_nt(kmean.astype(BF16), qb16)

    bidx = lax.broadcasted_iota(jnp.int32, (16, seq), 0)
    qblk = lax.shift_right_logical(lax.broadcasted_iota(jnp.int32, (16, seq), 1), MOBA_BLOCK_LOG2)
    rank = jnp.zeros((16, seq), F32)
    for n in range(nb):
        gn = gt[n:n + 1, :]
        beats = jnp.where(gn > gt, 1.0, jnp.where(gn == gt, jnp.where(bidx > n, 1.0, 0.0), 0.0))
        rank = rank + jnp.where(qblk > n, beats, 0.0)
    keep = jnp.where(bidx == qblk, 1.0,
                     jnp.where(bidx < qblk, jnp.where(rank < float(MOBA_TOPK), 1.0, 0.0), 0.0))
    bias_t = jnp.where(keep > 0.5, 0.0, -MASK_BIG)
    sigma = jnp.full((8, seq), slopes_ref[h] * float(HEAD_DIM ** 0.5), F32)
    s1, s2, s3 = [p.astype(F32) for p in _split3(sigma)]
    srow = lax.broadcasted_iota(jnp.int32, (8, seq), 0)
    sig_rows = jnp.where((srow == 0) | (srow == 3), s1,
                         jnp.where((srow == 1) | (srow == 4), s2,
                                   jnp.where((srow == 2) | (srow == 5), s3, 0.0)))
    bias_pad = jnp.concatenate([bias_t, sig_rows, jnp.zeros((HEAD_DIM - 24, seq), F32)], axis=0)
    bias_nat = bias_pad.T

    qa_ref[:, 0:HEAD_DIM] = qb16
    qa_ref[:, HEAD_DIM:2 * HEAD_DIM] = bias_nat.astype(BF16)
    kpos = lax.broadcasted_iota(jnp.int32, (seq, HEAD_DIM), 0)
    krow = lax.shift_right_logical(kpos, MOBA_BLOCK_LOG2)
    klane = lax.broadcasted_iota(jnp.int32, (seq, HEAD_DIM), 1)
    pos_lo = (kpos & (MOBA_BLOCK - 1)).astype(F32)
    pos_hi = (kpos - (kpos & (MOBA_BLOCK - 1))).astype(F32)
    kfeat = jnp.where(klane == krow, 1.0,
                      jnp.where((klane >= 16) & (klane < 19), pos_lo,
                                jnp.where((klane >= 19) & (klane < 22), pos_hi, 0.0)))
    ka_ref[:, 0:HEAD_DIM] = kb16
    ka_ref[:, HEAD_DIM:2 * HEAD_DIM] = kfeat.astype(BF16)
    _store_vt(vt_ref, v_ref[...])
    _attn_rows(qa_ref, ka_ref, vt_ref, o_ref)


def _fox_kernel(q_ref, k_ref, v_ref, gq_ref, gk_ref, fp_ref, wsrc_ref, o_ref, wdst_ref,
                qa_ref, ka_ref, vt_ref):
    h = pl.program_id(1)
    wdst_ref[...] = wsrc_ref[...].astype(BF16)
    qn = _rms(q_ref[...], gq_ref[...])
    kn = _rms(k_ref[...], gk_ref[...])
    packed = fp_ref[0]
    r = lax.broadcasted_iota(jnp.int32, (128, 128), 0)
    c = lax.broadcasted_iota(jnp.int32, (128, 128), 1)
    ones_row = r == 3 * FG_GROUP
    part = jnp.where(r == h, 0, jnp.where(r == FG_GROUP + h, 1, jnp.where(r == 2 * FG_GROUP + h, 2, -1)))
    pq = jnp.where(part == c, 1.0, jnp.where(ones_row & (c >= 3) & (c < 6), 1.0, 0.0)).astype(BF16)
    pk = jnp.where((part + 3 == c) & (part >= 0), -1.0, jnp.where(ones_row & (c < 3), 1.0, 0.0)).astype(BF16)
    qa_ref[:, 0:HEAD_DIM] = qn.astype(BF16)
    qa_ref[:, HEAD_DIM:2 * HEAD_DIM] = _dot(packed, pq).astype(BF16)
    ka_ref[:, 0:HEAD_DIM] = kn.astype(BF16)
    ka_ref[:, HEAD_DIM:2 * HEAD_DIM] = _dot(packed, pk).astype(BF16)
    _store_vt(vt_ref, v_ref[...])
    _attn_rows(qa_ref, ka_ref, vt_ref, o_ref)


def _attn_scratch(seq):
    return [
        pltpu.VMEM((seq, 2 * HEAD_DIM), BF16),
        pltpu.VMEM((seq, 2 * HEAD_DIM), BF16),
        pltpu.VMEM((HEAD_DIM + VT_PAD, seq), BF16),
    ]


def _cast_slab_specs(w, n_steps, step_of):
    rows, cols = w.shape
    slab = rows // n_steps
    assert slab * n_steps == rows and slab % 16 == 0
    spec = pl.BlockSpec((slab, cols), lambda *g: (step_of(*g), 0))
    return spec, spec, jax.ShapeDtypeStruct((rows, cols), BF16)


def _moba(proj, slopes, gq, gk, batch, seq, n_heads, col0, w_f32):
    hspec = lambda off: pl.BlockSpec((seq, HEAD_DIM), lambda b, h: (b, col0 + off + h))
    w_in_spec, w_out_spec, w_shape = _cast_slab_specs(w_f32, n_heads * batch, lambda b, h: b * n_heads + h)
    return pl.pallas_call(
        _moba_kernel,
        grid=(batch, n_heads),
        in_specs=[
            pl.BlockSpec(memory_space=pltpu.SMEM),
            hspec(0), hspec(n_heads), hspec(2 * n_heads),
            pl.BlockSpec((1, HEAD_DIM), lambda b, h: (0, 0)),
            pl.BlockSpec((1, HEAD_DIM), lambda b, h: (0, 0)),
            w_in_spec,
        ],
        out_specs=[pl.BlockSpec((seq, HEAD_DIM), lambda b, h: (b, h)), w_out_spec],
        out_shape=[jax.ShapeDtypeStruct((batch * seq, n_heads * HEAD_DIM), BF16), w_shape],
        scratch_shapes=_attn_scratch(seq),
        compiler_params=_cparams(2, 56),
        name="moba_attention",
    )(slopes, proj, proj, proj, gq, gk, w_f32)


def _fox(proj, fpack, gq, gk, batch, seq, n_heads, col0, w_f32):
    hspec = lambda off: pl.BlockSpec((seq, HEAD_DIM), lambda b, h: (b, col0 + off + h))
    w_in_spec, w_out_spec, w_shape = _cast_slab_specs(w_f32, n_heads * batch, lambda b, h: b * n_heads + h)
    return pl.pallas_call(
        _fox_kernel,
        grid=(batch, n_heads),
        in_specs=[
            hspec(0), hspec(n_heads), hspec(2 * n_heads),
            pl.BlockSpec((1, HEAD_DIM), lambda b, h: (0, 0)),
            pl.BlockSpec((1, HEAD_DIM), lambda b, h: (0, 0)),
            pl.BlockSpec((1, seq, 128), lambda b, h: (b, 0, 0)),
            w_in_spec,
        ],
        out_specs=[pl.BlockSpec((seq, HEAD_DIM), lambda b, h: (b, h)), w_out_spec],
        out_shape=[jax.ShapeDtypeStruct((batch * seq, n_heads * HEAD_DIM), BF16), w_shape],
        scratch_shapes=_attn_scratch(seq),
        compiler_params=_cparams(2, 56),
        name="fox_attention",
    )(proj, proj, proj, gq, gk, fpack, w_f32)


def _top16_sorted(st, want_rank=False):
    tt = st.shape[1]
    rid = lax.broadcasted_iota(jnp.int32, (PEER_TOPK, tt), 0)
    sv = jnp.zeros((PEER_TOPK, tt), F32)
    rank = jnp.full(st.shape, float(PEER_TOPK), F32) if want_rank else None
    work = st
    for r in range(PEER_TOPK):
        mx = jnp.max(work, axis=0, keepdims=True)
        sv = jnp.where(rid == r, mx, sv)
        hit = work == mx
        if want_rank:
            rank = jnp.where(hit, float(r), rank)
        if r + 1 < PEER_TOPK:
            work = jnp.where(hit, -jnp.inf, work)
    return (sv, rank) if want_rank else sv


def _peer_select_kernel(qp_ref, keys_ref, w_ref, r2_s, e2_s, k1_s, c1_s):
    tt = qp_ref.shape[0]
    n_lc = tt // 128
    b8 = lax.broadcasted_iota(jnp.int32, (8, tt), 0)
    inf = jnp.inf
    for h in range(PEER_HEADS):
        q1 = qp_ref[:, (2 * h) * PEER_HALF:(2 * h + 1) * PEER_HALF].astype(BF16)
        q2 = qp_ref[:, (2 * h + 1) * PEER_HALF:(2 * h + 2) * PEER_HALF].astype(BF16)
        s1 = _dot_nt(keys_ref[2 * h], q1)
        s2 = _dot_nt(keys_ref[2 * h + 1], q2)
        sv0 = _top16_sorted(s1)
        sv1, rank2 = _top16_sorted(s2, want_rank=True)
        sv1a, sv1b = sv1[0:8, :], sv1[8:16, :]
        row = lambda a: sv0[a:a + 1, :]
        cands = [(row(0) + sv1a, sv1a), (row(0) + sv1b, sv1b), (row(1) + sv1a, sv1a)]
        for a, nbv in ((2, 5), (3, 4), (4, 3), (5, 2), (6, 2), (7, 2)):
            cands.append((jnp.where(b8 < nbv, row(a) + sv1a, -inf), sv1a))
        top1 = sv1[0:1, :]
        hsum = sv0[8:16, :] + top1
        work = [g for g, _ in cands] + [hsum]
        tau = None
        for r in range(PEER_TOPK):
            mx = functools.reduce(jnp.maximum, work)
            tau = jnp.max(mx, axis=0, keepdims=True)
            if r + 1 < PEER_TOPK:
                work = [jnp.where(g == tau, -inf, g) for g in work]
        top = row(0) + top1
        zsum = jnp.zeros((1, tt), F32)
        for g in [g for g, _ in cands] + [hsum]:
            zsum = zsum + jnp.sum(jnp.where(g >= tau, jnp.exp(g - top), 0.0), axis=0, keepdims=True)
        inv_z = 1.0 / zsum
        cnt = lambda g: jnp.sum(jnp.where(g >= tau, 1.0, 0.0), axis=0, keepdims=True)
        k_rows = [cnt(cands[0][0]) + cnt(cands[1][0])] + [cnt(g) for g, _ in cands[2:]]
        k_hi = jnp.where(hsum >= tau, 1.0, 0.0)
        k1 = jnp.zeros(s1.shape, F32)
        for a in range(PEER_TOPK):
            ka = k_rows[a] if a < 8 else k_hi[a - 8:a - 7, :]
            k1 = jnp.where(s1 == row(a), ka, k1)
        e2 = jnp.exp(s2 - top1)
        c1 = jnp.exp(s1 - row(0)) * inv_z
        for lc in range(n_lc):
            cols = slice(lc * 128, (lc + 1) * 128)
            r2_s[h, lc] = rank2[:, cols].astype(BF16)
            e2_s[h, lc] = e2[:, cols].astype(BF16)
            k1_s[h, lc] = k1[:, cols]
            c1_s[h, lc] = c1[:, cols]

    def gate_rows(i, carry):
        for lc in range(n_lc):
            acc = None
            for h in range(PEER_HEADS):
                k1 = jnp.broadcast_to(k1_s[h, lc, pl.ds(i, 1), :], (16, 128)).astype(BF16)
                c1 = jnp.broadcast_to(c1_s[h, lc, pl.ds(i, 1), :], (16, 128)).astype(BF16)
                k1 = jnp.tile(k1, (PEER_NKEYS // 16, 1))
                c1 = jnp.tile(c1, (PEER_NKEYS // 16, 1))
                w = jnp.where(r2_s[h, lc] < k1, e2_s[h, lc] * c1, jnp.zeros_like(c1))
                acc = w if acc is None else acc + w
            w_ref[pl.ds(pl.multiple_of(i * PEER_NKEYS, PEER_NKEYS), PEER_NKEYS), lc * 128:(lc + 1) * 128] = acc
        return carry

    lax.fori_loop(0, PEER_NKEYS, gate_rows, 0)


def _peer_select(qp, keys16):
    t = qp.shape[0]
    tt = 256
    n_exp = PEER_NKEYS * PEER_NKEYS
    stage = pltpu.VMEM((PEER_HEADS, tt // 128, PEER_NKEYS, 128), F32)
    stage16 = pltpu.VMEM((PEER_HEADS, tt // 128, PEER_NKEYS, 128), BF16)
    return pl.pallas_call(
        _peer_select_kernel,
        grid=(t // tt,),
        in_specs=[
            pl.BlockSpec((tt, qp.shape[1]), lambda i: (i, 0)),
            pl.BlockSpec(keys16.shape, lambda i: (0, 0, 0)),
        ],
        out_specs=pl.BlockSpec((n_exp, tt), lambda i: (0, i)),
        out_shape=jax.ShapeDtypeStruct((n_exp, t), BF16),
        scratch_shapes=[stage16, stage16, stage, stage],
        compiler_params=_cparams(1, 48),
        name="peer_select",
    )(qp, keys16)


def _peer_dense_kernel(h_ref, u_ref, v_ref, w_ref, x_ref, mod_ref, o_ref):
    e = pl.program_id(1)

    @pl.when(e == 0)
    def _():
        o_ref[...] = jnp.zeros_like(o_ref)

    at = _dot_nt(u_ref[...], h_ref[...])
    gelu = 0.5 * at * (1.0 + lax.erf(at * float(np.sqrt(0.5))))
    wt = (w_ref[...].astype(F32) * gelu).astype(BF16)
    o_ref[...] += _dot_tn(wt, v_ref[...])

    @pl.when(e == pl.num_programs(1) - 1)
    def _():
        o_ref[...] = x_ref[...] + mod_ref[0, 5:6, :] * o_ref[...]


def _peer_dense(h2, u16, v16, wgt, x1, mod3, seq):
    t, d = h2.shape
    n_exp = u16.shape[0]
    tm = 512 if t % 512 == 0 else t
    tm = min(tm, seq)
    te = 512
    return pl.pallas_call(
        _peer_dense_kernel,
        grid=(t // tm, n_exp // te),
        in_specs=[
            pl.BlockSpec((tm, d), lambda i, e: (i, 0)),
            pl.BlockSpec((te, d), lambda i, e: (e, 0)),
            pl.BlockSpec((te, d), lambda i, e: (e, 0)),
            pl.BlockSpec((te, tm), lambda i, e: (e, i)),
            pl.BlockSpec((tm, d), lambda i, e: (i, 0)),
            pl.BlockSpec((1, N_MOD, d), lambda i, e: ((i * tm) // seq, 0, 0)),
        ],
        out_specs=pl.BlockSpec((tm, d), lambda i, e: (i, 0)),
        out_shape=jax.ShapeDtypeStruct((t, d), F32),
        compiler_params=_cparams(2, 60),
        name="peer_dense",
    )(h2, u16, v16, wgt, x1, mod3)


def _layer(x, c8, w_ada, b_ada, norm1_g, w_in, b_f, qn_m, kn_m, qn_f, kn_f, w_out, norm2_g,
           w_pq, sub_keys, peer_u, peer_v):
    b, s, d = x.shape
    t = b * s
    n_heads = d // HEAD_DIM
    n_moba = n_heads // 2
    n_fox = n_heads - n_moba
    d_moba = n_moba * HEAD_DIM
    d_fox = n_fox * HEAD_DIM
    d_qkv = 3 * d_moba + 3 * d_fox
    assert s % ATTN_TILE == 0 and ATTN_TILE % MOBA_BLOCK == 0 and s % 512 == 0
    assert n_fox <= FG_GROUP and s // MOBA_BLOCK <= 16 and b <= 8

    mod = _adaln(c8, w_ada, b_ada.reshape(1, -1))
    mod3 = mod[:b].reshape(b, N_MOD, d)

    w_in_t = jnp.swapaxes(w_in, 0, 1)
    w_fg_t = w_in_t[d_qkv:]
    wfg3 = jnp.zeros((128, d), F32)
    bfg3 = jnp.zeros((1, 128), F32)
    for g in range(3):
        wfg3 = wfg3.at[g * FG_GROUP:g * FG_GROUP + n_fox, :].set(w_fg_t)
        bfg3 = bfg3.at[0, g * FG_GROUP:g * FG_GROUP + n_fox].set(b_f)
    h1, fpack = _norm1(x, norm1_g.reshape(1, d), mod3, wfg3.astype(BF16), bfg3, n_fox)

    proj = _matmul(h1.reshape(t, d), w_in_t, d_qkv, F32, "in_proj", w_rows_are_outputs=True)

    slopes = 2.0 ** (-8.0 * jnp.arange(1, n_moba + 1, dtype=F32) / n_moba)
    o_moba, u16 = _moba(proj, slopes, qn_m.reshape(1, -1), kn_m.reshape(1, -1), b, s, n_moba, 0, peer_u)
    o_fox, v16 = _fox(proj, fpack, qn_f.reshape(1, -1), kn_f.reshape(1, -1), b, s, n_fox, 3 * n_moba, peer_v)

    x1 = _outproj(o_moba, o_fox, w_out, x.reshape(t, d), mod3, s).reshape(b, s, d)

    h2 = _norm2(x1, norm2_g.reshape(1, d), mod3).reshape(t, d)
    qp = _matmul(h2, w_pq, w_pq.shape[1], F32, "peer_query")
    keys16 = sub_keys.reshape(2 * PEER_HEADS, PEER_NKEYS, PEER_HALF).astype(BF16)
    wgt = _peer_select(qp, keys16)
    out = _peer_dense(h2, u16, v16, wgt, x1.reshape(t, d), mod3, s)
    return out.reshape(b, s, d)


def kernel(x, c, w_ada, b_ada, norm1_g, w_in, b_f, q_norm_moba, k_norm_moba, q_norm_fox, k_norm_fox,
           w_out, norm2_g, w_pq, peer_sub_keys, peer_u, peer_v):
    b = x.shape[0]
    c8 = jnp.zeros((8, c.shape[1]), F32).at[:b].set(c)
    for l in range(w_ada.shape[0]):
        x = _layer(x, c8, w_ada[l], b_ada[l], norm1_g[l], w_in[l], b_f[l], q_norm_moba[l], k_norm_moba[l],
                   q_norm_fox[l], k_norm_fox[l], w_out[l], norm2_g[l], w_pq[l], peer_sub_keys[l],
                   peer_u[l], peer_v[l])
    return x
```

```python
import functools

import jax
import jax.numpy as jnp
import numpy as np
from jax import lax
from jax.experimental import pallas as pl
from jax.experimental.pallas import tpu as pltpu

F32 = jnp.float32
BF16 = jnp.bfloat16

HEAD_DIM = 128
MOBA_BLOCK = 256
MOBA_BLOCK_LOG2 = 8
MOBA_TOPK = 3
PEER_HEADS = 8
PEER_NKEYS = 128
PEER_TOPK = 16
PEER_HALF = 128
N_MOD = 6
NORM_EPS = 1e-6
NEG_INF = -1e30
MASK_BIG = 2.0 ** 100
ATTN_TILE = 512
LOG2E = 1.4426950408889634
FG_GROUP = 16
VT_PAD = 16

MIB = 1024 * 1024


def _cparams(n_axes, vmem_mib):
    return pltpu.CompilerParams(
        dimension_semantics=("arbitrary",) * n_axes,
        vmem_limit_bytes=int(vmem_mib * MIB),
    )


def _dot(a, b):
    return jnp.dot(a, b, preferred_element_type=F32)


def _dot_nt(a, b):
    return lax.dot_general(a, b, (((1,), (1,)), ((), ())), preferred_element_type=F32)


def _dot_tn(a, b):
    return lax.dot_general(a, b, (((0,), (0,)), ((), ())), preferred_element_type=F32)


def _split3(x):
    p1 = x.astype(BF16)
    r1 = x - p1.astype(F32)
    p2 = r1.astype(BF16)
    r2 = r1 - p2.astype(F32)
    p3 = r2.astype(BF16)
    return p1, p2, p3


def _rms(x, g):
    ms = jnp.mean(x * x, axis=-1, keepdims=True)
    return x * lax.rsqrt(ms + NORM_EPS) * g


def _adaln_kernel(c_ref, w_ref, b_ref, o_ref):
    c = c_ref[...]
    ca = c * (1.0 / (1.0 + jnp.exp(-c)))
    o_ref[...] = _dot(ca.astype(BF16), w_ref[...].astype(BF16)) + b_ref[...]


def _adaln(c_pad, w, b_row):
    rows, d = c_pad.shape
    n = w.shape[1]
    tn = 512
    return pl.pallas_call(
        _adaln_kernel,
        grid=(n // tn,),
        in_specs=[
            pl.BlockSpec((rows, d), lambda j: (0, 0)),
            pl.BlockSpec((d, tn), lambda j: (0, j)),
            pl.BlockSpec((1, tn), lambda j: (0, j)),
        ],
        out_specs=pl.BlockSpec((rows, tn), lambda j: (0, j)),
        out_shape=jax.ShapeDtypeStruct((rows, n), F32),
        compiler_params=_cparams(1, 40),
        name="adaln",
    )(c_pad, w, b_row)


def _norm_mod(x_ref, g_ref, mod_ref, row0):
    xf = x_ref[0]
    y = _rms(xf, g_ref[...])
    sh = mod_ref[0, row0:row0 + 1, :]
    sc = mod_ref[0, row0 + 1:row0 + 2, :]
    return y * (1.0 + sc) + sh


def _norm1_kernel(x_ref, g_ref, mod_ref, wfg_ref, bfg_ref, h_ref, fp_ref, carry_ref, *, inv_scale, n_fox):
    s_idx = pl.program_id(1)
    hb = _norm_mod(x_ref, g_ref, mod_ref, 0).astype(BF16)
    h_ref[0] = hb
    ts = hb.shape[0]

    @pl.when(s_idx == 0)
    def _():
        carry_ref[...] = jnp.zeros_like(carry_ref)

    z = _dot_nt(hb, wfg_ref[...]) + bfg_ref[...]
    lf = jnp.minimum(z, 0.0) - jnp.log1p(jnp.exp(-jnp.abs(z)))
    r = lax.broadcasted_iota(jnp.int32, (ts, ts), 0)
    c = lax.broadcasted_iota(jnp.int32, (ts, ts), 1)
    tri = jnp.where(c <= r, 1.0, 0.0).astype(BF16)
    l1, l2, l3 = _split3(lf)
    f = (_dot(tri, l3) + _dot(tri, l2)) + _dot(tri, l1) + carry_ref[...]
    carry_ref[...] = f[ts - 1:ts, :]
    f1, f2, f3 = [p.astype(F32) for p in _split3(f * inv_scale)]
    lane = lax.broadcasted_iota(jnp.int32, f.shape, 1)
    packed = jnp.where(lane < FG_GROUP, f1,
              jnp.where(lane < 2 * FG_GROUP, f2,
               jnp.where(lane < 3 * FG_GROUP, f3,
                jnp.where(lane == 3 * FG_GROUP, 1.0, 0.0))))
    fp_ref[0] = packed.astype(BF16)


def _norm1(x, g_row, mod3, wfg3, bfg3, n_fox):
    b, s, d = x.shape
    ts = 512
    kern = functools.partial(_norm1_kernel, inv_scale=float(HEAD_DIM ** 0.5), n_fox=n_fox)
    return pl.pallas_call(
        kern,
        grid=(b, s // ts),
        in_specs=[
            pl.BlockSpec((1, ts, d), lambda i, j: (i, j, 0)),
            pl.BlockSpec((1, d), lambda i, j: (0, 0)),
            pl.BlockSpec((1, N_MOD, d), lambda i, j: (i, 0, 0)),
            pl.BlockSpec((128, d), lambda i, j: (0, 0)),
            pl.BlockSpec((1, 128), lambda i, j: (0, 0)),
        ],
        out_specs=[
            pl.BlockSpec((1, ts, d), lambda i, j: (i, j, 0)),
            pl.BlockSpec((1, ts, 128), lambda i, j: (i, j, 0)),
        ],
        out_shape=[
            jax.ShapeDtypeStruct((b, s, d), BF16),
            jax.ShapeDtypeStruct((b, s, 128), BF16),
        ],
        scratch_shapes=[pltpu.VMEM((1, 128), F32)],
        compiler_params=_cparams(2, 48),
        name="norm1_fgate",
    )(x, g_row, mod3, wfg3, bfg3)


def _norm2_kernel(x_ref, g_ref, mod_ref, w_ref, h_ref, q_ref):
    hb = _norm_mod(x_ref, g_ref, mod_ref, 3).astype(BF16)
    h_ref[0] = hb
    q_ref[...] = _dot(hb, w_ref[...])


def _norm2_query(x, g_row, mod3, wq16):
    b, s, d = x.shape
    nq = wq16.shape[1]
    ts = 512
    nj = s // ts
    return pl.pallas_call(
        _norm2_kernel,
        grid=(b, nj),
        in_specs=[
            pl.BlockSpec((1, ts, d), lambda i, j: (i, j, 0)),
            pl.BlockSpec((1, d), lambda i, j: (0, 0)),
            pl.BlockSpec((1, N_MOD, d), lambda i, j: (i, 0, 0)),
            pl.BlockSpec((d, nq), lambda i, j: (0, 0), pipeline_mode=pl.Buffered(1)),
        ],
        out_specs=[
            pl.BlockSpec((1, ts, d), lambda i, j: (i, j, 0)),
            pl.BlockSpec((ts, nq), lambda i, j: (i * nj + j, 0)),
        ],
        out_shape=[
            jax.ShapeDtypeStruct((b, s, d), BF16),
            jax.ShapeDtypeStruct((b * s, nq), F32),
        ],
        compiler_params=_cparams(2, 58),
        name="norm2_peer_query",
    )(x, g_row, mod3, wq16)


MM_TM = 1024
MM_TN = 512


def _cast_weight(w_ref, wb_ref):
    @pl.when(pl.program_id(1) == 0)
    def _():
        wb_ref[...] = w_ref[...].astype(BF16)


def _mm_kernel(a_ref, w_ref, o_ref, wb_ref, *, w_rows_are_outputs):
    _cast_weight(w_ref, wb_ref)
    dot = _dot_nt if w_rows_are_outputs else _dot
    o_ref[...] = dot(a_ref[...], wb_ref[...]).astype(o_ref.dtype)


def _matmul(a, w, n, out_dtype, name, w_rows_are_outputs=False):
    m, k = a.shape
    tm = MM_TM if m % MM_TM == 0 else m
    tn = MM_TN
    assert n % tn == 0
    if w_rows_are_outputs:
        w_block, w_spec = (tn, k), pl.BlockSpec((tn, k), lambda j, i: (j, 0))
    else:
        w_block, w_spec = (k, tn), pl.BlockSpec((k, tn), lambda j, i: (0, j))
    return pl.pallas_call(
        functools.partial(_mm_kernel, w_rows_are_outputs=w_rows_are_outputs),
        grid=(n // tn, m // tm),
        in_specs=[pl.BlockSpec((tm, k), lambda j, i: (i, 0)), w_spec],
        out_specs=pl.BlockSpec((tm, tn), lambda j, i: (i, j)),
        out_shape=jax.ShapeDtypeStruct((m, n), out_dtype),
        scratch_shapes=[pltpu.VMEM(w_block, BF16)],
        compiler_params=_cparams(2, 48),
        name=name,
    )(a, w)


def _outproj_kernel(oa_ref, ob_ref, w_ref, x_ref, mod_ref, o_ref, wb_ref):
    _cast_weight(w_ref, wb_ref)
    ka = oa_ref.shape[1]
    acc = _dot(oa_ref[...], wb_ref[0:ka, :]) + _dot(ob_ref[...], wb_ref[ka:, :])
    gate = mod_ref[0, 2:3, :]
    o_ref[...] = x_ref[...] + gate * acc


def _outproj(oa, ob, w, x2d, mod3, seq):
    m, ka = oa.shape
    kb = ob.shape[1]
    k, n = w.shape
    assert ka + kb == k
    tm = min(MM_TM if m % MM_TM == 0 else m, seq)
    tn = MM_TN
    return pl.pallas_call(
        _outproj_kernel,
        grid=(n // tn, m // tm),
        in_specs=[
            pl.BlockSpec((tm, ka), lambda j, i: (i, 0)),
            pl.BlockSpec((tm, kb), lambda j, i: (i, 0)),
            pl.BlockSpec((k, tn), lambda j, i: (0, j)),
            pl.BlockSpec((tm, tn), lambda j, i: (i, j)),
            pl.BlockSpec((1, N_MOD, tn), lambda j, i: ((i * tm) // seq, 0, j)),
        ],
        out_specs=pl.BlockSpec((tm, tn), lambda j, i: (i, j)),
        out_shape=jax.ShapeDtypeStruct((m, n), F32),
        scratch_shapes=[pltpu.VMEM((k, tn), BF16)],
        compiler_params=_cparams(2, 52),
        name="out_proj_residual",
    )(oa, ob, w, x2d, mod3)


def _store_vt(vt_ref, v):
    seq = v.shape[0]
    vt_ref[0:HEAD_DIM, :] = v.T.astype(BF16)
    rid = lax.broadcasted_iota(jnp.int32, (VT_PAD, seq), 0)
    vt_ref[HEAD_DIM:HEAD_DIM + VT_PAD, :] = jnp.where(rid == 0, 1.0, 0.0).astype(BF16)


def _attn_rows(qa_ref, ka_ref, vt_ref, o_ref):
    seq = qa_ref.shape[0]
    t = ATTN_TILE
    c_log2 = float(HEAD_DIM ** -0.5 * LOG2E)
    row = lax.broadcasted_iota(jnp.int32, (t, t), 0)
    col = lax.broadcasted_iota(jnp.int32, (t, t), 1)
    causal = row <= col
    for qp in range(seq // t):
        kv = (qp + 1) * t
        qa = qa_ref[qp * t:(qp + 1) * t, :]
        st = _dot_nt(ka_ref[0:kv, :], qa) * c_log2
        tail = jnp.where(causal, st[kv - t:kv, :], NEG_INF)
        m = jnp.max(tail, axis=0, keepdims=True)
        if qp > 0:
            head = st[0:kv - t, :]
            m = jnp.maximum(m, jnp.max(head, axis=0, keepdims=True))
            p_head = jnp.exp2(head - m)
        p_tail = jnp.exp2(tail - m)
        acc = _dot(vt_ref[:, kv - t:kv], p_tail.astype(BF16))
        if qp > 0:
            acc = acc + _dot(vt_ref[:, 0:kv - t], p_head.astype(BF16))
        ot = acc[0:HEAD_DIM, :] * (1.0 / acc[HEAD_DIM:HEAD_DIM + 1, :])
        o_ref[qp * t:(qp + 1) * t, :] = ot.T.astype(o_ref.dtype)


def _moba_kernel(slopes_ref, q_ref, k_ref, v_ref, gq_ref, gk_ref, wsrc_ref, o_ref, wdst_ref,
                 qa_ref, ka_ref, vt_ref):
    h = pl.program_id(1)
    wdst_ref[...] = wsrc_ref[...].astype(BF16)
    seq = q_ref.shape[0]
    nb = seq // MOBA_BLOCK
    qn = _rms(q_ref[...], gq_ref[...])
    kn = _rms(k_ref[...], gk_ref[...])
    qb16 = qn.astype(BF16)
    kb16 = kn.astype(BF16)

    rid = lax.broadcasted_iota(jnp.int32, (16, HEAD_DIM), 0)
    kmean = jnp.zeros((16, HEAD_DIM), F32)
    for n in range(nb):
        blk = jnp.sum(kn[n * MOBA_BLOCK:(n + 1) * MOBA_BLOCK, :], axis=0, keepdims=True) / float(MOBA_BLOCK)
        kmean = jnp.where(rid == n, blk, kmean)
    gt = _dot_nt(kmean.astype(BF16), qb16)

    bidx = lax.broadcasted_iota(jnp.int32, (16, seq), 0)
    qblk = lax.shift_right_logical(lax.broadcasted_iota(jnp.int32, (16, seq), 1), MOBA_BLOCK_LOG2)
    rank = jnp.zeros((16, seq), F32)
    for n in range(nb):
        gn = gt[n:n + 1, :]
        beats = jnp.where(gn > gt, 1.0, jnp.where(gn == gt, jnp.where(bidx > n, 1.0, 0.0), 0.0))
        rank = rank + jnp.where(qblk > n, beats, 0.0)
    keep = jnp.where(bidx == qblk, 1.0,
                     jnp.where(bidx < qblk, jnp.where(rank < float(MOBA_TOPK), 1.0, 0.0), 0.0))
    bias_t = jnp.where(keep > 0.5, 0.0, -MASK_BIG)
    sigma = jnp.full((8, seq), slopes_ref[h] * float(HEAD_DIM ** 0.5), F32)
    s1, s2, s3 = [p.astype(F32) for p in _split3(sigma)]
    srow = lax.broadcasted_iota(jnp.int32, (8, seq), 0)
    sig_rows = jnp.where((srow == 0) | (srow == 3), s1,
                         jnp.where((srow == 1) | (srow == 4), s2,
                                   jnp.where((srow == 2) | (srow == 5), s3, 0.0)))
    bias_pad = jnp.concatenate([bias_t, sig_rows, jnp.zeros((HEAD_DIM - 24, seq), F32)], axis=0)
    bias_nat = bias_pad.T

    qa_ref[:, 0:HEAD_DIM] = qb16
    qa_ref[:, HEAD_DIM:2 * HEAD_DIM] = bias_nat.astype(BF16)
    kpos = lax.broadcasted_iota(jnp.int32, (seq, HEAD_DIM), 0)
    krow = lax.shift_right_logical(kpos, MOBA_BLOCK_LOG2)
    klane = lax.broadcasted_iota(jnp.int32, (seq, HEAD_DIM), 1)
    pos_lo = (kpos & (MOBA_BLOCK - 1)).astype(F32)
    pos_hi = (kpos - (kpos & (MOBA_BLOCK - 1))).astype(F32)
    kfeat = jnp.where(klane == krow, 1.0,
                      jnp.where((klane >= 16) & (klane < 19), pos_lo,
                                jnp.where((klane >= 19) & (klane < 22), pos_hi, 0.0)))
    ka_ref[:, 0:HEAD_DIM] = kb16
    ka_ref[:, HEAD_DIM:2 * HEAD_DIM] = kfeat.astype(BF16)
    _store_vt(vt_ref, v_ref[...])
    _attn_rows(qa_ref, ka_ref, vt_ref, o_ref)


def _fox_kernel(q_ref, k_ref, v_ref, gq_ref, gk_ref, fp_ref, wsrc_ref, w2src_ref, o_ref, wdst_ref, w2dst_ref,
                qa_ref, ka_ref, vt_ref):
    h = pl.program_id(1)
    wdst_ref[...] = wsrc_ref[...].astype(BF16)
    w2dst_ref[...] = w2src_ref[...].astype(BF16)
    qn = _rms(q_ref[...], gq_ref[...])
    kn = _rms(k_ref[...], gk_ref[...])
    packed = fp_ref[0]
    r = lax.broadcasted_iota(jnp.int32, (128, 128), 0)
    c = lax.broadcasted_iota(jnp.int32, (128, 128), 1)
    ones_row = r == 3 * FG_GROUP
    part = jnp.where(r == h, 0, jnp.where(r == FG_GROUP + h, 1, jnp.where(r == 2 * FG_GROUP + h, 2, -1)))
    pq = jnp.where(part == c, 1.0, jnp.where(ones_row & (c >= 3) & (c < 6), 1.0, 0.0)).astype(BF16)
    pk = jnp.where((part + 3 == c) & (part >= 0), -1.0, jnp.where(ones_row & (c < 3), 1.0, 0.0)).astype(BF16)
    qa_ref[:, 0:HEAD_DIM] = qn.astype(BF16)
    qa_ref[:, HEAD_DIM:2 * HEAD_DIM] = _dot(packed, pq).astype(BF16)
    ka_ref[:, 0:HEAD_DIM] = kn.astype(BF16)
    ka_ref[:, HEAD_DIM:2 * HEAD_DIM] = _dot(packed, pk).astype(BF16)
    _store_vt(vt_ref, v_ref[...])
    _attn_rows(qa_ref, ka_ref, vt_ref, o_ref)


def _attn_scratch(seq):
    return [
        pltpu.VMEM((seq, 2 * HEAD_DIM), BF16),
        pltpu.VMEM((seq, 2 * HEAD_DIM), BF16),
        pltpu.VMEM((HEAD_DIM + VT_PAD, seq), BF16),
    ]


def _cast_slab_specs(w, n_steps, step_of):
    rows, cols = w.shape
    slab = rows // n_steps
    assert slab * n_steps == rows and slab % 16 == 0
    spec = pl.BlockSpec((slab, cols), lambda *g: (step_of(*g), 0))
    return spec, spec, jax.ShapeDtypeStruct((rows, cols), BF16)


def _moba(proj, slopes, gq, gk, batch, seq, n_heads, col0, w_f32):
    hspec = lambda off: pl.BlockSpec((seq, HEAD_DIM), lambda b, h: (b, col0 + off + h))
    w_in_spec, w_out_spec, w_shape = _cast_slab_specs(w_f32, n_heads * batch, lambda b, h: b * n_heads + h)
    return pl.pallas_call(
        _moba_kernel,
        grid=(batch, n_heads),
        in_specs=[
            pl.BlockSpec(memory_space=pltpu.SMEM),
            hspec(0), hspec(n_heads), hspec(2 * n_heads),
            pl.BlockSpec((1, HEAD_DIM), lambda b, h: (0, 0)),
            pl.BlockSpec((1, HEAD_DIM), lambda b, h: (0, 0)),
            w_in_spec,
        ],
        out_specs=[pl.BlockSpec((seq, HEAD_DIM), lambda b, h: (b, h)), w_out_spec],
        out_shape=[jax.ShapeDtypeStruct((batch * seq, n_heads * HEAD_DIM), BF16), w_shape],
        scratch_shapes=_attn_scratch(seq),
        compiler_params=_cparams(2, 56),
        name="moba_attention",
    )(slopes, proj, proj, proj, gq, gk, w_f32)


def _fox(proj, fpack, gq, gk, batch, seq, n_heads, col0, w_f32, w2_f32):
    hspec = lambda off: pl.BlockSpec((seq, HEAD_DIM), lambda b, h: (b, col0 + off + h))
    step_of = lambda b, h: b * n_heads + h
    w_in_spec, w_out_spec, w_shape = _cast_slab_specs(w_f32, n_heads * batch, step_of)
    w2_in_spec, w2_out_spec, w2_shape = _cast_slab_specs(w2_f32, n_heads * batch, step_of)
    return pl.pallas_call(
        _fox_kernel,
        grid=(batch, n_heads),
        in_specs=[
            hspec(0), hspec(n_heads), hspec(2 * n_heads),
            pl.BlockSpec((1, HEAD_DIM), lambda b, h: (0, 0)),
            pl.BlockSpec((1, HEAD_DIM), lambda b, h: (0, 0)),
            pl.BlockSpec((1, seq, 128), lambda b, h: (b, 0, 0)),
            w_in_spec, w2_in_spec,
        ],
        out_specs=[pl.BlockSpec((seq, HEAD_DIM), lambda b, h: (b, h)), w_out_spec, w2_out_spec],
        out_shape=[jax.ShapeDtypeStruct((batch * seq, n_heads * HEAD_DIM), BF16), w_shape, w2_shape],
        scratch_shapes=_attn_scratch(seq),
        compiler_params=_cparams(2, 56),
        name="fox_attention",
    )(proj, proj, proj, gq, gk, fpack, w_f32, w2_f32)


def _top16_sorted(st, want_rank=False):
    tt = st.shape[1]
    rid = lax.broadcasted_iota(jnp.int32, (PEER_TOPK, tt), 0)
    sv = jnp.zeros((PEER_TOPK, tt), F32)
    rank = jnp.full(st.shape, float(PEER_TOPK), F32) if want_rank else None
    work = st
    for r in range(PEER_TOPK):
        mx = jnp.max(work, axis=0, keepdims=True)
        sv = jnp.where(rid == r, mx, sv)
        hit = work == mx
        if want_rank:
            rank = jnp.where(hit, float(r), rank)
        if r + 1 < PEER_TOPK:
            work = jnp.where(hit, -jnp.inf, work)
    return (sv, rank) if want_rank else sv


def _peer_select_kernel(qp_ref, keys_ref, w_ref, r2_s, e2_s, k1_s, c1_s):
    tt = qp_ref.shape[0]
    n_lc = tt // 128
    b8 = lax.broadcasted_iota(jnp.int32, (8, tt), 0)
    inf = jnp.inf
    for h in range(PEER_HEADS):
        q1 = qp_ref[:, (2 * h) * PEER_HALF:(2 * h + 1) * PEER_HALF].astype(BF16)
        q2 = qp_ref[:, (2 * h + 1) * PEER_HALF:(2 * h + 2) * PEER_HALF].astype(BF16)
        s1 = _dot_nt(keys_ref[2 * h], q1)
        s2 = _dot_nt(keys_ref[2 * h + 1], q2)
        sv0 = _top16_sorted(s1)
        sv1, rank2 = _top16_sorted(s2, want_rank=True)
        sv1a, sv1b = sv1[0:8, :], sv1[8:16, :]
        row = lambda a: sv0[a:a + 1, :]
        cands = [(row(0) + sv1a, sv1a), (row(0) + sv1b, sv1b), (row(1) + sv1a, sv1a)]
        for a, nbv in ((2, 5), (3, 4), (4, 3), (5, 2), (6, 2), (7, 2)):
            cands.append((jnp.where(b8 < nbv, row(a) + sv1a, -inf), sv1a))
        top1 = sv1[0:1, :]
        hsum = sv0[8:16, :] + top1
        work = [g for g, _ in cands] + [hsum]
        tau = None
        for r in range(PEER_TOPK):
            mx = functools.reduce(jnp.maximum, work)
            tau = jnp.max(mx, axis=0, keepdims=True)
            if r + 1 < PEER_TOPK:
                work = [jnp.where(g == tau, -inf, g) for g in work]
        top = row(0) + top1
        zsum = jnp.zeros((1, tt), F32)
        for g in [g for g, _ in cands] + [hsum]:
            zsum = zsum + jnp.sum(jnp.where(g >= tau, jnp.exp(g - top), 0.0), axis=0, keepdims=True)
        inv_z = 1.0 / zsum
        cnt = lambda g: jnp.sum(jnp.where(g >= tau, 1.0, 0.0), axis=0, keepdims=True)
        k_rows = [cnt(cands[0][0]) + cnt(cands[1][0])] + [cnt(g) for g, _ in cands[2:]]
        k_hi = jnp.where(hsum >= tau, 1.0, 0.0)
        k1 = jnp.zeros(s1.shape, F32)
        for a in range(PEER_TOPK):
            ka = k_rows[a] if a < 8 else k_hi[a - 8:a - 7, :]
            k1 = jnp.where(s1 == row(a), ka, k1)
        e2 = jnp.exp(s2 - top1)
        c1 = jnp.exp(s1 - row(0)) * inv_z
        for lc in range(n_lc):
            cols = slice(lc * 128, (lc + 1) * 128)
            r2_s[h, lc] = rank2[:, cols].astype(BF16)
            e2_s[h, lc] = e2[:, cols].astype(BF16)
            k1_s[h, lc] = k1[:, cols]
            c1_s[h, lc] = c1[:, cols]

    def gate_rows(i, carry):
        for lc in range(n_lc):
            acc = None
            for h in range(PEER_HEADS):
                k1 = jnp.broadcast_to(k1_s[h, lc, pl.ds(i, 1), :], (16, 128)).astype(BF16)
                c1 = jnp.broadcast_to(c1_s[h, lc, pl.ds(i, 1), :], (16, 128)).astype(BF16)
                k1 = jnp.tile(k1, (PEER_NKEYS // 16, 1))
                c1 = jnp.tile(c1, (PEER_NKEYS // 16, 1))
                w = jnp.where(r2_s[h, lc] < k1, e2_s[h, lc] * c1, jnp.zeros_like(c1))
                acc = w if acc is None else acc + w
            w_ref[pl.ds(pl.multiple_of(i * PEER_NKEYS, PEER_NKEYS), PEER_NKEYS), lc * 128:(lc + 1) * 128] = acc
        return carry

    lax.fori_loop(0, PEER_NKEYS, gate_rows, 0)


def _peer_select(qp, keys16):
    t = qp.shape[0]
    tt = 256
    n_exp = PEER_NKEYS * PEER_NKEYS
    stage = pltpu.VMEM((PEER_HEADS, tt // 128, PEER_NKEYS, 128), F32)
    stage16 = pltpu.VMEM((PEER_HEADS, tt // 128, PEER_NKEYS, 128), BF16)
    return pl.pallas_call(
        _peer_select_kernel,
        grid=(t // tt,),
        in_specs=[
            pl.BlockSpec((tt, qp.shape[1]), lambda i: (i, 0)),
            pl.BlockSpec(keys16.shape, lambda i: (0, 0, 0)),
        ],
        out_specs=pl.BlockSpec((n_exp, tt), lambda i: (0, i)),
        out_shape=jax.ShapeDtypeStruct((n_exp, t), BF16),
        scratch_shapes=[stage16, stage16, stage, stage],
        compiler_params=_cparams(1, 48),
        name="peer_select",
    )(qp, keys16)


def _peer_dense_kernel(h_ref, u_ref, v_ref, w_ref, x_ref, mod_ref, o_ref):
    e = pl.program_id(1)

    @pl.when(e == 0)
    def _():
        o_ref[...] = jnp.zeros_like(o_ref)

    at = _dot_nt(u_ref[...], h_ref[...])
    gelu = 0.5 * at * (1.0 + lax.erf(at * float(np.sqrt(0.5))))
    wt = (w_ref[...].astype(F32) * gelu).astype(BF16)
    o_ref[...] += _dot_tn(wt, v_ref[...])

    @pl.when(e == pl.num_programs(1) - 1)
    def _():
        o_ref[...] = x_ref[...] + mod_ref[0, 5:6, :] * o_ref[...]


def _peer_dense(h2, u16, v16, wgt, x1, mod3, seq):
    t, d = h2.shape
    n_exp = u16.shape[0]
    tm = 512 if t % 512 == 0 else t
    tm = min(tm, seq)
    te = 512
    return pl.pallas_call(
        _peer_dense_kernel,
        grid=(t // tm, n_exp // te),
        in_specs=[
            pl.BlockSpec((tm, d), lambda i, e: (i, 0)),
            pl.BlockSpec((te, d), lambda i, e: (e, 0)),
            pl.BlockSpec((te, d), lambda i, e: (e, 0)),
            pl.BlockSpec((te, tm), lambda i, e: (e, i)),
            pl.BlockSpec((tm, d), lambda i, e: (i, 0)),
            pl.BlockSpec((1, N_MOD, d), lambda i, e: ((i * tm) // seq, 0, 0)),
        ],
        out_specs=pl.BlockSpec((tm, d), lambda i, e: (i, 0)),
        out_shape=jax.ShapeDtypeStruct((t, d), F32),
        compiler_params=_cparams(2, 60),
        name="peer_dense",
    )(h2, u16, v16, wgt, x1, mod3)


def _layer(x, c8, w_ada, b_ada, norm1_g, w_in, b_f, qn_m, kn_m, qn_f, kn_f, w_out, norm2_g,
           w_pq, sub_keys, peer_u, peer_v):
    b, s, d = x.shape
    t = b * s
    n_heads = d // HEAD_DIM
    n_moba = n_heads // 2
    n_fox = n_heads - n_moba
    d_moba = n_moba * HEAD_DIM
    d_fox = n_fox * HEAD_DIM
    d_qkv = 3 * d_moba + 3 * d_fox
    assert s % ATTN_TILE == 0 and ATTN_TILE % MOBA_BLOCK == 0 and s % 512 == 0
    assert n_fox <= FG_GROUP and s // MOBA_BLOCK <= 16 and b <= 8

    mod = _adaln(c8, w_ada, b_ada.reshape(1, -1))
    mod3 = mod[:b].reshape(b, N_MOD, d)

    w_in_t = jnp.swapaxes(w_in, 0, 1)
    w_fg_t = w_in_t[d_qkv:]
    wfg3 = jnp.zeros((128, d), F32)
    bfg3 = jnp.zeros((1, 128), F32)
    for g in range(3):
        wfg3 = wfg3.at[g * FG_GROUP:g * FG_GROUP + n_fox, :].set(w_fg_t)
        bfg3 = bfg3.at[0, g * FG_GROUP:g * FG_GROUP + n_fox].set(b_f)
    h1, fpack = _norm1(x, norm1_g.reshape(1, d), mod3, wfg3.astype(BF16), bfg3, n_fox)

    proj = _matmul(h1.reshape(t, d), w_in_t, d_qkv, F32, "in_proj", w_rows_are_outputs=True)

    slopes = 2.0 ** (-8.0 * jnp.arange(1, n_moba + 1, dtype=F32) / n_moba)
    o_moba, u16 = _moba(proj, slopes, qn_m.reshape(1, -1), kn_m.reshape(1, -1), b, s, n_moba, 0, peer_u)
    o_fox, v16, wq16 = _fox(proj, fpack, qn_f.reshape(1, -1), kn_f.reshape(1, -1), b, s, n_fox, 3 * n_moba,
                            peer_v, w_pq)

    x1 = _outproj(o_moba, o_fox, w_out, x.reshape(t, d), mod3, s).reshape(b, s, d)

    h2, qp = _norm2_query(x1, norm2_g.reshape(1, d), mod3, wq16)
    h2 = h2.reshape(t, d)
    keys16 = sub_keys.reshape(2 * PEER_HEADS, PEER_NKEYS, PEER_HALF).astype(BF16)
    wgt = _peer_select(qp, keys16)
    out = _peer_dense(h2, u16, v16, wgt, x1.reshape(t, d), mod3, s)
    return out.reshape(b, s, d)


def kernel(x, c, w_ada, b_ada, norm1_g, w_in, b_f, q_norm_moba, k_norm_moba, q_norm_fox, k_norm_fox,
           w_out, norm2_g, w_pq, peer_sub_keys, peer_u, peer_v):
    b = x.shape[0]
    c8 = jnp.zeros((8, c.shape[1]), F32).at[:b].set(c)
    for l in range(w_ada.shape[0]):
        x = _layer(x, c8, w_ada[l], b_ada[l], norm1_g[l], w_in[l], b_f[l], q_norm_moba[l], k_norm_moba[l],
                   q_norm_fox[l], k_norm_fox[l], w_out[l], norm2_g[l], w_pq[l], peer_sub_keys[l],
                   peer_u[l], peer_v[l])
    return x
```

```python
import functools

import jax
import jax.numpy as jnp
import numpy as np
from jax import lax
from jax.experimental import pallas as pl
from jax.experimental.pallas import tpu as pltpu

F32 = jnp.float32
BF16 = jnp.bfloat16

HEAD_DIM = 128
MOBA_BLOCK = 256
MOBA_BLOCK_LOG2 = 8
MOBA_TOPK = 3
PEER_HEADS = 8
PEER_NKEYS = 128
PEER_TOPK = 16
PEER_HALF = 128
N_MOD = 6
NORM_EPS = 1e-6
NEG_INF = -1e30
MASK_BIG = 2.0 ** 100
ATTN_TILE = 512
LOG2E = 1.4426950408889634
FG_GROUP = 16
VT_PAD = 16
HEADS_PER_STEP = 2

MIB = 1024 * 1024


def _cparams(n_axes, vmem_mib):
    return pltpu.CompilerParams(
        dimension_semantics=("arbitrary",) * n_axes,
        vmem_limit_bytes=int(vmem_mib * MIB),
    )


def _dot(a, b):
    return jnp.dot(a, b, preferred_element_type=F32)


def _dot_nt(a, b):
    return lax.dot_general(a, b, (((1,), (1,)), ((), ())), preferred_element_type=F32)


def _dot_tn(a, b):
    return lax.dot_general(a, b, (((0,), (0,)), ((), ())), preferred_element_type=F32)


def _split3(x):
    p1 = x.astype(BF16)
    r1 = x - p1.astype(F32)
    p2 = r1.astype(BF16)
    r2 = r1 - p2.astype(F32)
    p3 = r2.astype(BF16)
    return p1, p2, p3


def _rms(x, g):
    ms = jnp.mean(x * x, axis=-1, keepdims=True)
    return x * lax.rsqrt(ms + NORM_EPS) * g


def _adaln_kernel(c_ref, w_ref, b_ref, o_ref):
    c = c_ref[...]
    ca = c * (1.0 / (1.0 + jnp.exp(-c)))
    o_ref[...] = _dot(ca.astype(BF16), w_ref[...].astype(BF16)) + b_ref[...]


def _adaln(c_pad, w, b_row):
    rows, d = c_pad.shape
    n = w.shape[1]
    tn = 512
    return pl.pallas_call(
        _adaln_kernel,
        grid=(n // tn,),
        in_specs=[
            pl.BlockSpec((rows, d), lambda j: (0, 0)),
            pl.BlockSpec((d, tn), lambda j: (0, j)),
            pl.BlockSpec((1, tn), lambda j: (0, j)),
        ],
        out_specs=pl.BlockSpec((rows, tn), lambda j: (0, j)),
        out_shape=jax.ShapeDtypeStruct((rows, n), F32),
        compiler_params=_cparams(1, 40),
        name="adaln",
    )(c_pad, w, b_row)


def _norm_mod(x_ref, g_ref, mod_ref, row0):
    xf = x_ref[0]
    y = _rms(xf, g_ref[...])
    sh = mod_ref[0, row0:row0 + 1, :]
    sc = mod_ref[0, row0 + 1:row0 + 2, :]
    return y * (1.0 + sc) + sh


def _norm1_kernel(x_ref, g_ref, mod_ref, wfg_ref, bfg_ref, h_ref, fp_ref, carry_ref, *, inv_scale, n_fox):
    s_idx = pl.program_id(1)
    hb = _norm_mod(x_ref, g_ref, mod_ref, 0).astype(BF16)
    h_ref[0] = hb
    ts = hb.shape[0]

    @pl.when(s_idx == 0)
    def _():
        carry_ref[...] = jnp.zeros_like(carry_ref)

    z = _dot_nt(hb, wfg_ref[...]) + bfg_ref[...]
    lf = jnp.minimum(z, 0.0) - jnp.log1p(jnp.exp(-jnp.abs(z)))
    r = lax.broadcasted_iota(jnp.int32, (ts, ts), 0)
    c = lax.broadcasted_iota(jnp.int32, (ts, ts), 1)
    tri = jnp.where(c <= r, 1.0, 0.0).astype(BF16)
    l1, l2, l3 = _split3(lf)
    f = (_dot(tri, l3) + _dot(tri, l2)) + _dot(tri, l1) + carry_ref[...]
    carry_ref[...] = f[ts - 1:ts, :]
    f1, f2, f3 = [p.astype(F32) for p in _split3(f * inv_scale)]
    lane = lax.broadcasted_iota(jnp.int32, f.shape, 1)
    packed = jnp.where(lane < FG_GROUP, f1,
              jnp.where(lane < 2 * FG_GROUP, f2,
               jnp.where(lane < 3 * FG_GROUP, f3,
                jnp.where(lane == 3 * FG_GROUP, 1.0, 0.0))))
    fp_ref[0] = packed.astype(BF16)


def _norm1(x, g_row, mod3, wfg3, bfg3, n_fox):
    b, s, d = x.shape
    ts = 512
    kern = functools.partial(_norm1_kernel, inv_scale=float(HEAD_DIM ** 0.5), n_fox=n_fox)
    return pl.pallas_call(
        kern,
        grid=(b, s // ts),
        in_specs=[
            pl.BlockSpec((1, ts, d), lambda i, j: (i, j, 0)),
            pl.BlockSpec((1, d), lambda i, j: (0, 0)),
            pl.BlockSpec((1, N_MOD, d), lambda i, j: (i, 0, 0)),
            pl.BlockSpec((128, d), lambda i, j: (0, 0)),
            pl.BlockSpec((1, 128), lambda i, j: (0, 0)),
        ],
        out_specs=[
            pl.BlockSpec((1, ts, d), lambda i, j: (i, j, 0)),
            pl.BlockSpec((1, ts, 128), lambda i, j: (i, j, 0)),
        ],
        out_shape=[
            jax.ShapeDtypeStruct((b, s, d), BF16),
            jax.ShapeDtypeStruct((b, s, 128), BF16),
        ],
        scratch_shapes=[pltpu.VMEM((1, 128), F32)],
        compiler_params=_cparams(2, 48),
        name="norm1_fgate",
    )(x, g_row, mod3, wfg3, bfg3)


def _norm2_kernel(x_ref, g_ref, mod_ref, w_ref, h_ref, q_ref):
    hb = _norm_mod(x_ref, g_ref, mod_ref, 3).astype(BF16)
    h_ref[0] = hb
    q_ref[...] = _dot(hb, w_ref[...])


def _norm2_query(x, g_row, mod3, wq16):
    b, s, d = x.shape
    nq = wq16.shape[1]
    ts = 512
    nj = s // ts
    return pl.pallas_call(
        _norm2_kernel,
        grid=(b, nj),
        in_specs=[
            pl.BlockSpec((1, ts, d), lambda i, j: (i, j, 0)),
            pl.BlockSpec((1, d), lambda i, j: (0, 0)),
            pl.BlockSpec((1, N_MOD, d), lambda i, j: (i, 0, 0)),
            pl.BlockSpec((d, nq), lambda i, j: (0, 0), pipeline_mode=pl.Buffered(1)),
        ],
        out_specs=[
            pl.BlockSpec((1, ts, d), lambda i, j: (i, j, 0)),
            pl.BlockSpec((ts, nq), lambda i, j: (i * nj + j, 0)),
        ],
        out_shape=[
            jax.ShapeDtypeStruct((b, s, d), BF16),
            jax.ShapeDtypeStruct((b * s, nq), F32),
        ],
        compiler_params=_cparams(2, 58),
        name="norm2_peer_query",
    )(x, g_row, mod3, wq16)


MM_TM = 1024
MM_TN = 512


def _cast_weight(w_ref, wb_ref):
    @pl.when(pl.program_id(1) == 0)
    def _():
        wb_ref[...] = w_ref[...].astype(BF16)


def _mm_kernel(a_ref, w_ref, o_ref, wb_ref, *, w_rows_are_outputs):
    _cast_weight(w_ref, wb_ref)
    dot = _dot_nt if w_rows_are_outputs else _dot
    o_ref[...] = dot(a_ref[...], wb_ref[...]).astype(o_ref.dtype)


def _matmul(a, w, n, out_dtype, name, w_rows_are_outputs=False):
    m, k = a.shape
    tm = MM_TM if m % MM_TM == 0 else m
    tn = MM_TN
    assert n % tn == 0
    if w_rows_are_outputs:
        w_block, w_spec = (tn, k), pl.BlockSpec((tn, k), lambda j, i: (j, 0))
    else:
        w_block, w_spec = (k, tn), pl.BlockSpec((k, tn), lambda j, i: (0, j))
    return pl.pallas_call(
        functools.partial(_mm_kernel, w_rows_are_outputs=w_rows_are_outputs),
        grid=(n // tn, m // tm),
        in_specs=[pl.BlockSpec((tm, k), lambda j, i: (i, 0)), w_spec],
        out_specs=pl.BlockSpec((tm, tn), lambda j, i: (i, j)),
        out_shape=jax.ShapeDtypeStruct((m, n), out_dtype),
        scratch_shapes=[pltpu.VMEM(w_block, BF16)],
        compiler_params=_cparams(2, 48),
        name=name,
    )(a, w)


def _outproj_kernel(oa_ref, ob_ref, w_ref, x_ref, mod_ref, o_ref, wb_ref):
    _cast_weight(w_ref, wb_ref)
    ka = oa_ref.shape[1]
    acc = _dot(oa_ref[...], wb_ref[0:ka, :]) + _dot(ob_ref[...], wb_ref[ka:, :])
    gate = mod_ref[0, 2:3, :]
    o_ref[...] = x_ref[...] + gate * acc


def _outproj(oa, ob, w, x2d, mod3, seq):
    m, ka = oa.shape
    kb = ob.shape[1]
    k, n = w.shape
    assert ka + kb == k
    tm = min(MM_TM if m % MM_TM == 0 else m, seq)
    tn = MM_TN
    return pl.pallas_call(
        _outproj_kernel,
        grid=(n // tn, m // tm),
        in_specs=[
            pl.BlockSpec((tm, ka), lambda j, i: (i, 0)),
            pl.BlockSpec((tm, kb), lambda j, i: (i, 0)),
            pl.BlockSpec((k, tn), lambda j, i: (0, j)),
            pl.BlockSpec((tm, tn), lambda j, i: (i, j)),
            pl.BlockSpec((1, N_MOD, tn), lambda j, i: ((i * tm) // seq, 0, j)),
        ],
        out_specs=pl.BlockSpec((tm, tn), lambda j, i: (i, j)),
        out_shape=jax.ShapeDtypeStruct((m, n), F32),
        scratch_shapes=[pltpu.VMEM((k, tn), BF16)],
        compiler_params=_cparams(2, 52),
        name="out_proj_residual",
    )(oa, ob, w, x2d, mod3)


def _store_vt(vt_ref, v):
    seq = v.shape[0]
    vt_ref[0:HEAD_DIM, :] = v.T.astype(BF16)
    rid = lax.broadcasted_iota(jnp.int32, (VT_PAD, seq), 0)
    vt_ref[HEAD_DIM:HEAD_DIM + VT_PAD, :] = jnp.where(rid == 0, 1.0, 0.0).astype(BF16)


def _attn_rows(qa_ref, ka_ref, vt_ref, o_ref):
    seq = qa_ref.shape[0]
    t = ATTN_TILE
    c_log2 = float(HEAD_DIM ** -0.5 * LOG2E)
    row = lax.broadcasted_iota(jnp.int32, (t, t), 0)
    col = lax.broadcasted_iota(jnp.int32, (t, t), 1)
    causal = row <= col
    for qp in range(seq // t):
        kv = (qp + 1) * t
        qa = qa_ref[qp * t:(qp + 1) * t, :]
        st = _dot_nt(ka_ref[0:kv, :], qa) * c_log2
        tail = jnp.where(causal, st[kv - t:kv, :], NEG_INF)
        m = jnp.max(tail, axis=0, keepdims=True)
        if qp > 0:
            head = st[0:kv - t, :]
            m = jnp.maximum(m, jnp.max(head, axis=0, keepdims=True))
            p_head = jnp.exp2(head - m)
        p_tail = jnp.exp2(tail - m)
        acc = _dot(vt_ref[:, kv - t:kv], p_tail.astype(BF16))
        if qp > 0:
            acc = acc + _dot(vt_ref[:, 0:kv - t], p_head.astype(BF16))
        ot = acc[0:HEAD_DIM, :] * (1.0 / acc[HEAD_DIM:HEAD_DIM + 1, :])
        o_ref[qp * t:(qp + 1) * t, :] = ot.T.astype(o_ref.dtype)


def _moba_kernel(slopes_ref, q_ref, k_ref, v_ref, gq_ref, gk_ref, wsrc_ref, o_ref, wdst_ref, *scratch):
    wdst_ref[...] = wsrc_ref[...].astype(BF16)
    for hh in range(HEADS_PER_STEP):
        cols = slice(hh * HEAD_DIM, (hh + 1) * HEAD_DIM)
        _moba_head(slopes_ref[pl.program_id(1) * HEADS_PER_STEP + hh], q_ref[:, cols], k_ref[:, cols], v_ref[:, cols],
                   gq_ref, gk_ref, o_ref.at[:, cols], *scratch[3 * hh:3 * hh + 3])


def _moba_head(slope, q, k, v, gq_ref, gk_ref, o_ref, qa_ref, ka_ref, vt_ref):
    seq = q.shape[0]
    nb = seq // MOBA_BLOCK
    qn = _rms(q, gq_ref[...])
    kn = _rms(k, gk_ref[...])
    qb16 = qn.astype(BF16)
    kb16 = kn.astype(BF16)

    rid = lax.broadcasted_iota(jnp.int32, (16, HEAD_DIM), 0)
    kmean = jnp.zeros((16, HEAD_DIM), F32)
    for n in range(nb):
        blk = jnp.sum(kn[n * MOBA_BLOCK:(n + 1) * MOBA_BLOCK, :], axis=0, keepdims=True) / float(MOBA_BLOCK)
        kmean = jnp.where(rid == n, blk, kmean)
    gt = _dot_nt(kmean.astype(BF16), qb16)

    bidx = lax.broadcasted_iota(jnp.int32, (16, seq), 0)
    qblk = lax.shift_right_logical(lax.broadcasted_iota(jnp.int32, (16, seq), 1), MOBA_BLOCK_LOG2)
    rank = jnp.zeros((16, seq), F32)
    for n in range(nb):
        gn = gt[n:n + 1, :]
        beats = jnp.where(gn > gt, 1.0, jnp.where(gn == gt, jnp.where(bidx > n, 1.0, 0.0), 0.0))
        rank = rank + jnp.where(qblk > n, beats, 0.0)
    keep = jnp.where(bidx == qblk, 1.0,
                     jnp.where(bidx < qblk, jnp.where(rank < float(MOBA_TOPK), 1.0, 0.0), 0.0))
    bias_t = jnp.where(keep > 0.5, 0.0, -MASK_BIG)
    sigma = jnp.full((8, seq), slope * float(HEAD_DIM ** 0.5), F32)
    s1, s2, s3 = [p.astype(F32) for p in _split3(sigma)]
    srow = lax.broadcasted_iota(jnp.int32, (8, seq), 0)
    sig_rows = jnp.where((srow == 0) | (srow == 3), s1,
                         jnp.where((srow == 1) | (srow == 4), s2,
                                   jnp.where((srow == 2) | (srow == 5), s3, 0.0)))
    bias_pad = jnp.concatenate([bias_t, sig_rows, jnp.zeros((HEAD_DIM - 24, seq), F32)], axis=0)
    bias_nat = bias_pad.T

    qa_ref[:, 0:HEAD_DIM] = qb16
    qa_ref[:, HEAD_DIM:2 * HEAD_DIM] = bias_nat.astype(BF16)
    kpos = lax.broadcasted_iota(jnp.int32, (seq, HEAD_DIM), 0)
    krow = lax.shift_right_logical(kpos, MOBA_BLOCK_LOG2)
    klane = lax.broadcasted_iota(jnp.int32, (seq, HEAD_DIM), 1)
    pos_lo = (kpos & (MOBA_BLOCK - 1)).astype(F32)
    pos_hi = (kpos - (kpos & (MOBA_BLOCK - 1))).astype(F32)
    kfeat = jnp.where(klane == krow, 1.0,
                      jnp.where((klane >= 16) & (klane < 19), pos_lo,
                                jnp.where((klane >= 19) & (klane < 22), pos_hi, 0.0)))
    ka_ref[:, 0:HEAD_DIM] = kb16
    ka_ref[:, HEAD_DIM:2 * HEAD_DIM] = kfeat.astype(BF16)
    _store_vt(vt_ref, v)
    _attn_rows(qa_ref, ka_ref, vt_ref, o_ref)


def _fox_kernel(q_ref, k_ref, v_ref, gq_ref, gk_ref, fp_ref, wsrc_ref, w2src_ref, o_ref, wdst_ref, w2dst_ref,
                *scratch):
    wdst_ref[...] = wsrc_ref[...].astype(BF16)
    w2dst_ref[...] = w2src_ref[...].astype(BF16)
    for hh in range(HEADS_PER_STEP):
        cols = slice(hh * HEAD_DIM, (hh + 1) * HEAD_DIM)
        _fox_head(pl.program_id(1) * HEADS_PER_STEP + hh, q_ref[:, cols], k_ref[:, cols], v_ref[:, cols],
                  gq_ref, gk_ref, fp_ref, o_ref.at[:, cols], *scratch[3 * hh:3 * hh + 3])


def _fox_head(h, q, k, v, gq_ref, gk_ref, fp_ref, o_ref, qa_ref, ka_ref, vt_ref):
    qn = _rms(q, gq_ref[...])
    kn = _rms(k, gk_ref[...])
    packed = fp_ref[0]
    r = lax.broadcasted_iota(jnp.int32, (128, 128), 0)
    c = lax.broadcasted_iota(jnp.int32, (128, 128), 1)
    ones_row = r == 3 * FG_GROUP
    part = jnp.where(r == h, 0, jnp.where(r == FG_GROUP + h, 1, jnp.where(r == 2 * FG_GROUP + h, 2, -1)))
    pq = jnp.where(part == c, 1.0, jnp.where(ones_row & (c >= 3) & (c < 6), 1.0, 0.0)).astype(BF16)
    pk = jnp.where((part + 3 == c) & (part >= 0), -1.0, jnp.where(ones_row & (c < 3), 1.0, 0.0)).astype(BF16)
    qa_ref[:, 0:HEAD_DIM] = qn.astype(BF16)
    qa_ref[:, HEAD_DIM:2 * HEAD_DIM] = _dot(packed, pq).astype(BF16)
    ka_ref[:, 0:HEAD_DIM] = kn.astype(BF16)
    ka_ref[:, HEAD_DIM:2 * HEAD_DIM] = _dot(packed, pk).astype(BF16)
    _store_vt(vt_ref, v)
    _attn_rows(qa_ref, ka_ref, vt_ref, o_ref)


def _attn_scratch(seq):
    return HEADS_PER_STEP * [
        pltpu.VMEM((seq, 2 * HEAD_DIM), BF16),
        pltpu.VMEM((seq, 2 * HEAD_DIM), BF16),
        pltpu.VMEM((HEAD_DIM + VT_PAD, seq), BF16),
    ]


def _cast_slab_specs(w, n_steps, step_of):
    rows, cols = w.shape
    slab = rows // n_steps
    assert slab * n_steps == rows and slab % 16 == 0
    spec = pl.BlockSpec((slab, cols), lambda *g: (step_of(*g), 0))
    return spec, spec, jax.ShapeDtypeStruct((rows, cols), BF16)


def _moba(proj, slopes, gq, gk, batch, seq, n_heads, col0, w_f32):
    hps = HEADS_PER_STEP
    assert n_heads % hps == 0 and col0 % hps == 0
    n_hp = n_heads // hps
    hspec = lambda off: pl.BlockSpec((seq, hps * HEAD_DIM), lambda b, h: (b, (col0 + off) // hps + h))
    w_in_spec, w_out_spec, w_shape = _cast_slab_specs(w_f32, n_hp * batch, lambda b, h: b * n_hp + h)
    return pl.pallas_call(
        _moba_kernel,
        grid=(batch, n_hp),
        in_specs=[
            pl.BlockSpec(memory_space=pltpu.SMEM),
            hspec(0), hspec(n_heads), hspec(2 * n_heads),
            pl.BlockSpec((1, HEAD_DIM), lambda b, h: (0, 0)),
            pl.BlockSpec((1, HEAD_DIM), lambda b, h: (0, 0)),
            w_in_spec,
        ],
        out_specs=[pl.BlockSpec((seq, hps * HEAD_DIM), lambda b, h: (b, h)), w_out_spec],
        out_shape=[jax.ShapeDtypeStruct((batch * seq, n_heads * HEAD_DIM), BF16), w_shape],
        scratch_shapes=_attn_scratch(seq),
        compiler_params=_cparams(2, 56),
        name="moba_attention",
    )(slopes, proj, proj, proj, gq, gk, w_f32)


def _fox(proj, fpack, gq, gk, batch, seq, n_heads, col0, w_f32, w2_f32):
    hps = HEADS_PER_STEP
    assert n_heads % hps == 0 and col0 % hps == 0
    n_hp = n_heads // hps
    hspec = lambda off: pl.BlockSpec((seq, hps * HEAD_DIM), lambda b, h: (b, (col0 + off) // hps + h))
    step_of = lambda b, h: b * n_hp + h
    w_in_spec, w_out_spec, w_shape = _cast_slab_specs(w_f32, n_hp * batch, step_of)
    w2_in_spec, w2_out_spec, w2_shape = _cast_slab_specs(w2_f32, n_hp * batch, step_of)
    return pl.pallas_call(
        _fox_kernel,
        grid=(batch, n_hp),
        in_specs=[
            hspec(0), hspec(n_heads), hspec(2 * n_heads),
            pl.BlockSpec((1, HEAD_DIM), lambda b, h: (0, 0)),
            pl.BlockSpec((1, HEAD_DIM), lambda b, h: (0, 0)),
            pl.BlockSpec((1, seq, 128), lambda b, h: (b, 0, 0)),
            w_in_spec, w2_in_spec,
        ],
        out_specs=[pl.BlockSpec((seq, hps * HEAD_DIM), lambda b, h: (b, h)), w_out_spec, w2_out_spec],
        out_shape=[jax.ShapeDtypeStruct((batch * seq, n_heads * HEAD_DIM), BF16), w_shape, w2_shape],
        scratch_shapes=_attn_scratch(seq),
        compiler_params=_cparams(2, 56),
        name="fox_attention",
    )(proj, proj, proj, gq, gk, fpack, w_f32, w2_f32)


def _top16_sorted(st, want_rank=False):
    tt = st.shape[1]
    rid = lax.broadcasted_iota(jnp.int32, (PEER_TOPK, tt), 0)
    sv = jnp.zeros((PEER_TOPK, tt), F32)
    rank = jnp.full(st.shape, float(PEER_TOPK), F32) if want_rank else None
    work = st
    for r in range(PEER_TOPK):
        mx = jnp.max(work, axis=0, keepdims=True)
        sv = jnp.where(rid == r, mx, sv)
        hit = work == mx
        if want_rank:
            rank = jnp.where(hit, float(r), rank)
        if r + 1 < PEER_TOPK:
            work = jnp.where(hit, -jnp.inf, work)
    return (sv, rank) if want_rank else sv


def _peer_select_kernel(qp_ref, keys_ref, w_ref, r2_s, e2_s, k1_s, c1_s):
    tt = qp_ref.shape[0]
    n_lc = tt // 128
    b8 = lax.broadcasted_iota(jnp.int32, (8, tt), 0)
    inf = jnp.inf
    for h in range(PEER_HEADS):
        q1 = qp_ref[:, (2 * h) * PEER_HALF:(2 * h + 1) * PEER_HALF].astype(BF16)
        q2 = qp_ref[:, (2 * h + 1) * PEER_HALF:(2 * h + 2) * PEER_HALF].astype(BF16)
        s1 = _dot_nt(keys_ref[2 * h], q1)
        s2 = _dot_nt(keys_ref[2 * h + 1], q2)
        sv0 = _top16_sorted(s1)
        sv1, rank2 = _top16_sorted(s2, want_rank=True)
        sv1a, sv1b = sv1[0:8, :], sv1[8:16, :]
        row = lambda a: sv0[a:a + 1, :]
        cands = [(row(0) + sv1a, sv1a), (row(0) + sv1b, sv1b), (row(1) + sv1a, sv1a)]
        for a, nbv in ((2, 5), (3, 4), (4, 3), (5, 2), (6, 2), (7, 2)):
            cands.append((jnp.where(b8 < nbv, row(a) + sv1a, -inf), sv1a))
        top1 = sv1[0:1, :]
        hsum = sv0[8:16, :] + top1
        work = [g for g, _ in cands] + [hsum]
        tau = None
        for r in range(PEER_TOPK):
            mx = functools.reduce(jnp.maximum, work)
            tau = jnp.max(mx, axis=0, keepdims=True)
            if r + 1 < PEER_TOPK:
                work = [jnp.where(g == tau, -inf, g) for g in work]
        top = row(0) + top1
        zsum = jnp.zeros((1, tt), F32)
        for g in [g for g, _ in cands] + [hsum]:
            zsum = zsum + jnp.sum(jnp.where(g >= tau, jnp.exp(g - top), 0.0), axis=0, keepdims=True)
        inv_z = 1.0 / zsum
        cnt = lambda g: jnp.sum(jnp.where(g >= tau, 1.0, 0.0), axis=0, keepdims=True)
        k_rows = [cnt(cands[0][0]) + cnt(cands[1][0])] + [cnt(g) for g, _ in cands[2:]]
        k_hi = jnp.where(hsum >= tau, 1.0, 0.0)
        k1 = jnp.zeros(s1.shape, F32)
        for a in range(PEER_TOPK):
            ka = k_rows[a] if a < 8 else k_hi[a - 8:a - 7, :]
            k1 = jnp.where(s1 == row(a), ka, k1)
        e2 = jnp.exp(s2 - top1)
        c1 = jnp.exp(s1 - row(0)) * inv_z
        for lc in range(n_lc):
            cols = slice(lc * 128, (lc + 1) * 128)
            r2_s[h, lc] = rank2[:, cols].astype(BF16)
            e2_s[h, lc] = e2[:, cols].astype(BF16)
            k1_s[h, lc] = k1[:, cols]
            c1_s[h, lc] = c1[:, cols]

    def gate_rows(i, carry):
        for lc in range(n_lc):
            acc = None
            for h in range(PEER_HEADS):
                k1 = jnp.broadcast_to(k1_s[h, lc, pl.ds(i, 1), :], (16, 128)).astype(BF16)
                c1 = jnp.broadcast_to(c1_s[h, lc, pl.ds(i, 1), :], (16, 128)).astype(BF16)
                k1 = jnp.tile(k1, (PEER_NKEYS // 16, 1))
                c1 = jnp.tile(c1, (PEER_NKEYS // 16, 1))
                w = jnp.where(r2_s[h, lc] < k1, e2_s[h, lc] * c1, jnp.zeros_like(c1))
                acc = w if acc is None else acc + w
            w_ref[pl.ds(pl.multiple_of(i * PEER_NKEYS, PEER_NKEYS), PEER_NKEYS), lc * 128:(lc + 1) * 128] = acc
        return carry

    lax.fori_loop(0, PEER_NKEYS, gate_rows, 0)


def _peer_select(qp, keys16):
    t = qp.shape[0]
    tt = 256
    n_exp = PEER_NKEYS * PEER_NKEYS
    stage = pltpu.VMEM((PEER_HEADS, tt // 128, PEER_NKEYS, 128), F32)
    stage16 = pltpu.VMEM((PEER_HEADS, tt // 128, PEER_NKEYS, 128), BF16)
    return pl.pallas_call(
        _peer_select_kernel,
        grid=(t // tt,),
        in_specs=[
            pl.BlockSpec((tt, qp.shape[1]), lambda i: (i, 0)),
            pl.BlockSpec(keys16.shape, lambda i: (0, 0, 0)),
        ],
        out_specs=pl.BlockSpec((n_exp, tt), lambda i: (0, i)),
        out_shape=jax.ShapeDtypeStruct((n_exp, t), BF16),
        scratch_shapes=[stage16, stage16, stage, stage],
        compiler_params=_cparams(1, 48),
        name="peer_select",
    )(qp, keys16)


def _peer_dense_kernel(h_ref, u_ref, v_ref, w_ref, x_ref, mod_ref, o_ref):
    e = pl.program_id(1)

    @pl.when(e == 0)
    def _():
        o_ref[...] = jnp.zeros_like(o_ref)

    at = _dot_nt(u_ref[...], h_ref[...])
    gelu = 0.5 * at * (1.0 + lax.erf(at * float(np.sqrt(0.5))))
    wt = (w_ref[...].astype(F32) * gelu).astype(BF16)
    o_ref[...] += _dot_tn(wt, v_ref[...])

    @pl.when(e == pl.num_programs(1) - 1)
    def _():
        o_ref[...] = x_ref[...] + mod_ref[0, 5:6, :] * o_ref[...]


def _peer_dense(h2, u16, v16, wgt, x1, mod3, seq):
    t, d = h2.shape
    n_exp = u16.shape[0]
    tm = 512 if t % 512 == 0 else t
    tm = min(tm, seq)
    te = 512
    return pl.pallas_call(
        _peer_dense_kernel,
        grid=(t // tm, n_exp // te),
        in_specs=[
            pl.BlockSpec((tm, d), lambda i, e: (i, 0)),
            pl.BlockSpec((te, d), lambda i, e: (e, 0)),
            pl.BlockSpec((te, d), lambda i, e: (e, 0)),
            pl.BlockSpec((te, tm), lambda i, e: (e, i)),
            pl.BlockSpec((tm, d), lambda i, e: (i, 0)),
            pl.BlockSpec((1, N_MOD, d), lambda i, e: ((i * tm) // seq, 0, 0)),
        ],
        out_specs=pl.BlockSpec((tm, d), lambda i, e: (i, 0)),
        out_shape=jax.ShapeDtypeStruct((t, d), F32),
        compiler_params=_cparams(2, 60),
        name="peer_dense",
    )(h2, u16, v16, wgt, x1, mod3)


def _layer(x, c8, w_ada, b_ada, norm1_g, w_in, b_f, qn_m, kn_m, qn_f, kn_f, w_out, norm2_g,
           w_pq, sub_keys, peer_u, peer_v):
    b, s, d = x.shape
    t = b * s
    n_heads = d // HEAD_DIM
    n_moba = n_heads // 2
    n_fox = n_heads - n_moba
    d_moba = n_moba * HEAD_DIM
    d_fox = n_fox * HEAD_DIM
    d_qkv = 3 * d_moba + 3 * d_fox
    assert s % ATTN_TILE == 0 and ATTN_TILE % MOBA_BLOCK == 0 and s % 512 == 0
    assert n_fox <= FG_GROUP and s // MOBA_BLOCK <= 16 and b <= 8

    mod = _adaln(c8, w_ada, b_ada.reshape(1, -1))
    mod3 = mod[:b].reshape(b, N_MOD, d)

    w_in_t = jnp.swapaxes(w_in, 0, 1)
    w_fg_t = w_in_t[d_qkv:]
    wfg3 = jnp.zeros((128, d), F32)
    bfg3 = jnp.zeros((1, 128), F32)
    for g in range(3):
        wfg3 = wfg3.at[g * FG_GROUP:g * FG_GROUP + n_fox, :].set(w_fg_t)
        bfg3 = bfg3.at[0, g * FG_GROUP:g * FG_GROUP + n_fox].set(b_f)
    h1, fpack = _norm1(x, norm1_g.reshape(1, d), mod3, wfg3.astype(BF16), bfg3, n_fox)

    proj = _matmul(h1.reshape(t, d), w_in_t, d_qkv, F32, "in_proj", w_rows_are_outputs=True)

    slopes = 2.0 ** (-8.0 * jnp.arange(1, n_moba + 1, dtype=F32) / n_moba)
    o_moba, u16 = _moba(proj, slopes, qn_m.reshape(1, -1), kn_m.reshape(1, -1), b, s, n_moba, 0, peer_u)
    o_fox, v16, wq16 = _fox(proj, fpack, qn_f.reshape(1, -1), kn_f.reshape(1, -1), b, s, n_fox, 3 * n_moba,
                            peer_v, w_pq)

    x1 = _outproj(o_moba, o_fox, w_out, x.reshape(t, d), mod3, s).reshape(b, s, d)

    h2, qp = _norm2_query(x1, norm2_g.reshape(1, d), mod3, wq16)
    h2 = h2.reshape(t, d)
    keys16 = sub_keys.reshape(2 * PEER_HEADS, PEER_NKEYS, PEER_HALF).astype(BF16)
    wgt = _peer_select(qp, keys16)
    out = _peer_dense(h2, u16, v16, wgt, x1.reshape(t, d), mod3, s)
    return out.reshape(b, s, d)


def kernel(x, c, w_ada, b_ada, norm1_g, w_in, b_f, q_norm_moba, k_norm_moba, q_norm_fox, k_norm_fox,
           w_out, norm2_g, w_pq, peer_sub_keys, peer_u, peer_v):
    b = x.shape[0]
    c8 = jnp.zeros((8, c.shape[1]), F32).at[:b].set(c)
    for l in range(w_ada.shape[0]):
        x = _layer(x, c8, w_ada[l], b_ada[l], norm1_g[l], w_in[l], b_f[l], q_norm_moba[l], k_norm_moba[l],
                   q_norm_fox[l], k_norm_fox[l], w_out[l], norm2_g[l], w_pq[l], peer_sub_keys[l],
                   peer_u[l], peer_v[l])
    return x
```

```python
import functools

import jax
import jax.numpy as jnp
import numpy as np
from jax import lax
from jax.experimental import pallas as pl
from jax.experimental.pallas import tpu as pltpu

F32 = jnp.float32
BF16 = jnp.bfloat16

HEAD_DIM = 128
MOBA_BLOCK = 256
MOBA_BLOCK_LOG2 = 8
MOBA_TOPK = 3
PEER_HEADS = 8
PEER_NKEYS = 128
PEER_TOPK = 16
PEER_HALF = 128
N_MOD = 6
NORM_EPS = 1e-6
NEG_INF = -1e30
MASK_BIG = 2.0 ** 100
ATTN_TILE = 512
LOG2E = 1.4426950408889634
FG_GROUP = 16
VT_PAD = 16
HEADS_PER_STEP = 2

MIB = 1024 * 1024


def _cparams(n_axes, vmem_mib):
    return pltpu.CompilerParams(
        dimension_semantics=("arbitrary",) * n_axes,
        vmem_limit_bytes=int(vmem_mib * MIB),
    )


def _dot(a, b):
    return jnp.dot(a, b, preferred_element_type=F32)


def _dot_nt(a, b):
    return lax.dot_general(a, b, (((1,), (1,)), ((), ())), preferred_element_type=F32)


def _dot_tn(a, b):
    return lax.dot_general(a, b, (((0,), (0,)), ((), ())), preferred_element_type=F32)


def _split3(x):
    p1 = x.astype(BF16)
    r1 = x - p1.astype(F32)
    p2 = r1.astype(BF16)
    r2 = r1 - p2.astype(F32)
    p3 = r2.astype(BF16)
    return p1, p2, p3


def _rms(x, g):
    ms = jnp.mean(x * x, axis=-1, keepdims=True)
    return x * lax.rsqrt(ms + NORM_EPS) * g


def _adaln_kernel(c_ref, w_ref, b_ref, o_ref):
    c = c_ref[...]
    ca = c * (1.0 / (1.0 + jnp.exp(-c)))
    o_ref[...] = _dot(ca.astype(BF16), w_ref[...].astype(BF16)) + b_ref[...]


def _adaln(c_pad, w, b_row):
    rows, d = c_pad.shape
    n = w.shape[1]
    tn = 512
    return pl.pallas_call(
        _adaln_kernel,
        grid=(n // tn,),
        in_specs=[
            pl.BlockSpec((rows, d), lambda j: (0, 0)),
            pl.BlockSpec((d, tn), lambda j: (0, j)),
            pl.BlockSpec((1, tn), lambda j: (0, j)),
        ],
        out_specs=pl.BlockSpec((rows, tn), lambda j: (0, j)),
        out_shape=jax.ShapeDtypeStruct((rows, n), F32),
        compiler_params=_cparams(1, 40),
        name="adaln",
    )(c_pad, w, b_row)


def _norm_mod(x_ref, g_ref, mod_ref, row0):
    xf = x_ref[0]
    y = _rms(xf, g_ref[...])
    sh = mod_ref[0, row0:row0 + 1, :]
    sc = mod_ref[0, row0 + 1:row0 + 2, :]
    return y * (1.0 + sc) + sh


def _norm1_kernel(x_ref, g_ref, mod_ref, wfg_ref, bfg_ref, h_ref, fp_ref, carry_ref, *, inv_scale, n_fox):
    s_idx = pl.program_id(1)
    hb = _norm_mod(x_ref, g_ref, mod_ref, 0).astype(BF16)
    h_ref[0] = hb
    ts = hb.shape[0]

    @pl.when(s_idx == 0)
    def _():
        carry_ref[...] = jnp.zeros_like(carry_ref)

    z = _dot_nt(hb, wfg_ref[...]) + bfg_ref[...]
    lf = jnp.minimum(z, 0.0) - jnp.log1p(jnp.exp(-jnp.abs(z)))
    r = lax.broadcasted_iota(jnp.int32, (ts, ts), 0)
    c = lax.broadcasted_iota(jnp.int32, (ts, ts), 1)
    tri = jnp.where(c <= r, 1.0, 0.0).astype(BF16)
    l1, l2, l3 = _split3(lf)
    f = (_dot(tri, l3) + _dot(tri, l2)) + _dot(tri, l1) + carry_ref[...]
    carry_ref[...] = f[ts - 1:ts, :]
    f1, f2, f3 = [p.astype(F32) for p in _split3(f * inv_scale)]
    lane = lax.broadcasted_iota(jnp.int32, f.shape, 1)
    packed = jnp.where(lane < FG_GROUP, f1,
              jnp.where(lane < 2 * FG_GROUP, f2,
               jnp.where(lane < 3 * FG_GROUP, f3,
                jnp.where(lane == 3 * FG_GROUP, 1.0, 0.0))))
    fp_ref[0] = packed.astype(BF16)


def _norm1(x, g_row, mod3, wfg3, bfg3, n_fox):
    b, s, d = x.shape
    ts = 512
    kern = functools.partial(_norm1_kernel, inv_scale=float(HEAD_DIM ** 0.5), n_fox=n_fox)
    return pl.pallas_call(
        kern,
        grid=(b, s // ts),
        in_specs=[
            pl.BlockSpec((1, ts, d), lambda i, j: (i, j, 0)),
            pl.BlockSpec((1, d), lambda i, j: (0, 0)),
            pl.BlockSpec((1, N_MOD, d), lambda i, j: (i, 0, 0)),
            pl.BlockSpec((128, d), lambda i, j: (0, 0)),
            pl.BlockSpec((1, 128), lambda i, j: (0, 0)),
        ],
        out_specs=[
            pl.BlockSpec((1, ts, d), lambda i, j: (i, j, 0)),
            pl.BlockSpec((1, ts, 128), lambda i, j: (i, j, 0)),
        ],
        out_shape=[
            jax.ShapeDtypeStruct((b, s, d), BF16),
            jax.ShapeDtypeStruct((b, s, 128), BF16),
        ],
        scratch_shapes=[pltpu.VMEM((1, 128), F32)],
        compiler_params=_cparams(2, 48),
        name="norm1_fgate",
    )(x, g_row, mod3, wfg3, bfg3)


def _norm2_kernel(x_ref, g_ref, mod_ref, w_ref, h_ref, q_ref):
    hb = _norm_mod(x_ref, g_ref, mod_ref, 3).astype(BF16)
    h_ref[0] = hb
    q_ref[...] = _dot(hb, w_ref[...])


def _norm2_query(x, g_row, mod3, wq16):
    b, s, d = x.shape
    nq = wq16.shape[1]
    ts = 512
    nj = s // ts
    return pl.pallas_call(
        _norm2_kernel,
        grid=(b, nj),
        in_specs=[
            pl.BlockSpec((1, ts, d), lambda i, j: (i, j, 0)),
            pl.BlockSpec((1, d), lambda i, j: (0, 0)),
            pl.BlockSpec((1, N_MOD, d), lambda i, j: (i, 0, 0)),
            pl.BlockSpec((d, nq), lambda i, j: (0, 0), pipeline_mode=pl.Buffered(1)),
        ],
        out_specs=[
            pl.BlockSpec((1, ts, d), lambda i, j: (i, j, 0)),
            pl.BlockSpec((ts, nq), lambda i, j: (i * nj + j, 0)),
        ],
        out_shape=[
            jax.ShapeDtypeStruct((b, s, d), BF16),
            jax.ShapeDtypeStruct((b * s, nq), F32),
        ],
        compiler_params=_cparams(2, 58),
        name="norm2_peer_query",
    )(x, g_row, mod3, wq16)


MM_TM = 1024
MM_TN = 512


def _cast_weight(w_ref, wb_ref):
    @pl.when(pl.program_id(1) == 0)
    def _():
        wb_ref[...] = w_ref[...].astype(BF16)


def _mm_kernel(a_ref, w_ref, o_ref, wb_ref, *, w_rows_are_outputs):
    _cast_weight(w_ref, wb_ref)
    dot = _dot_nt if w_rows_are_outputs else _dot
    o_ref[...] = dot(a_ref[...], wb_ref[...]).astype(o_ref.dtype)


def _matmul(a, w, n, out_dtype, name, w_rows_are_outputs=False):
    m, k = a.shape
    tm = MM_TM if m % MM_TM == 0 else m
    tn = MM_TN
    assert n % tn == 0
    if w_rows_are_outputs:
        w_block, w_spec = (tn, k), pl.BlockSpec((tn, k), lambda j, i: (j, 0))
    else:
        w_block, w_spec = (k, tn), pl.BlockSpec((k, tn), lambda j, i: (0, j))
    return pl.pallas_call(
        functools.partial(_mm_kernel, w_rows_are_outputs=w_rows_are_outputs),
        grid=(n // tn, m // tm),
        in_specs=[pl.BlockSpec((tm, k), lambda j, i: (i, 0)), w_spec],
        out_specs=pl.BlockSpec((tm, tn), lambda j, i: (i, j)),
        out_shape=jax.ShapeDtypeStruct((m, n), out_dtype),
        scratch_shapes=[pltpu.VMEM(w_block, BF16)],
        compiler_params=_cparams(2, 48),
        name=name,
    )(a, w)


def _outproj_kernel(oa_ref, ob_ref, w_ref, x_ref, mod_ref, o_ref, wb_ref):
    _cast_weight(w_ref, wb_ref)
    ka = oa_ref.shape[1]
    acc = _dot(oa_ref[...], wb_ref[0:ka, :]) + _dot(ob_ref[...], wb_ref[ka:, :])
    gate = mod_ref[0, 2:3, :]
    o_ref[...] = x_ref[...] + gate * acc


def _outproj(oa, ob, w, x2d, mod3, seq):
    m, ka = oa.shape
    kb = ob.shape[1]
    k, n = w.shape
    assert ka + kb == k
    tm = min(MM_TM if m % MM_TM == 0 else m, seq)
    tn = MM_TN
    return pl.pallas_call(
        _outproj_kernel,
        grid=(n // tn, m // tm),
        in_specs=[
            pl.BlockSpec((tm, ka), lambda j, i: (i, 0)),
            pl.BlockSpec((tm, kb), lambda j, i: (i, 0)),
            pl.BlockSpec((k, tn), lambda j, i: (0, j)),
            pl.BlockSpec((tm, tn), lambda j, i: (i, j)),
            pl.BlockSpec((1, N_MOD, tn), lambda j, i: ((i * tm) // seq, 0, j)),
        ],
        out_specs=pl.BlockSpec((tm, tn), lambda j, i: (i, j)),
        out_shape=jax.ShapeDtypeStruct((m, n), F32),
        scratch_shapes=[pltpu.VMEM((k, tn), BF16)],
        compiler_params=_cparams(2, 52),
        name="out_proj_residual",
    )(oa, ob, w, x2d, mod3)


def _store_vt(vt_ref, v):
    seq = v.shape[0]
    vt_ref[0:HEAD_DIM, :] = v.T.astype(BF16)
    rid = lax.broadcasted_iota(jnp.int32, (VT_PAD, seq), 0)
    vt_ref[HEAD_DIM:HEAD_DIM + VT_PAD, :] = jnp.where(rid == 0, 1.0, 0.0).astype(BF16)


def _attn_rows(qa_ref, ka_ref, vt_ref, o_ref):
    seq = qa_ref.shape[0]
    t = ATTN_TILE
    c_log2 = float(HEAD_DIM ** -0.5 * LOG2E)
    row = lax.broadcasted_iota(jnp.int32, (t, t), 0)
    col = lax.broadcasted_iota(jnp.int32, (t, t), 1)
    causal = row <= col
    for qp in range(seq // t):
        kv = (qp + 1) * t
        qa = qa_ref[qp * t:(qp + 1) * t, :]
        st = _dot_nt(ka_ref[0:kv, :], qa) * c_log2
        tail = jnp.where(causal, st[kv - t:kv, :], NEG_INF)
        m = jnp.max(tail, axis=0, keepdims=True)
        if qp > 0:
            head = st[0:kv - t, :]
            m = jnp.maximum(m, jnp.max(head, axis=0, keepdims=True))
            p_head = jnp.exp2(head - m)
        p_tail = jnp.exp2(tail - m)
        acc = _dot(vt_ref[:, kv - t:kv], p_tail.astype(BF16))
        if qp > 0:
            acc = acc + _dot(vt_ref[:, 0:kv - t], p_head.astype(BF16))
        ot = acc[0:HEAD_DIM, :] * (1.0 / acc[HEAD_DIM:HEAD_DIM + 1, :])
        o_ref[qp * t:(qp + 1) * t, :] = ot.T.astype(o_ref.dtype)


def _moba_kernel(slopes_ref, q_ref, k_ref, v_ref, gq_ref, gk_ref, wsrc_ref, o_ref, wdst_ref, *scratch):
    wdst_ref[...] = wsrc_ref[...].astype(BF16)
    for hh in range(HEADS_PER_STEP):
        cols = slice(hh * HEAD_DIM, (hh + 1) * HEAD_DIM)
        _moba_head(slopes_ref[pl.program_id(1) * HEADS_PER_STEP + hh], q_ref[:, cols], k_ref[:, cols], v_ref[:, cols],
                   gq_ref, gk_ref, o_ref.at[:, cols], *scratch[3 * hh:3 * hh + 3])


def _moba_head(slope, q, k, v, gq_ref, gk_ref, o_ref, qa_ref, ka_ref, vt_ref):
    seq = q.shape[0]
    nb = seq // MOBA_BLOCK
    qn = _rms(q, gq_ref[...])
    kn = _rms(k, gk_ref[...])
    qb16 = qn.astype(BF16)
    kb16 = kn.astype(BF16)

    rid = lax.broadcasted_iota(jnp.int32, (16, HEAD_DIM), 0)
    kmean = jnp.zeros((16, HEAD_DIM), F32)
    for n in range(nb):
        blk = jnp.sum(kn[n * MOBA_BLOCK:(n + 1) * MOBA_BLOCK, :], axis=0, keepdims=True) / float(MOBA_BLOCK)
        kmean = jnp.where(rid == n, blk, kmean)
    gt = _dot_nt(kmean.astype(BF16), qb16)

    bidx = lax.broadcasted_iota(jnp.int32, (16, seq), 0)
    qblk = lax.shift_right_logical(lax.broadcasted_iota(jnp.int32, (16, seq), 1), MOBA_BLOCK_LOG2)
    rank = jnp.zeros((16, seq), F32)
    for n in range(nb):
        gn = gt[n:n + 1, :]
        beats = jnp.where(gn > gt, 1.0, jnp.where(gn == gt, jnp.where(bidx > n, 1.0, 0.0), 0.0))
        rank = rank + jnp.where(qblk > n, beats, 0.0)
    keep = jnp.where(bidx == qblk, 1.0,
                     jnp.where(bidx < qblk, jnp.where(rank < float(MOBA_TOPK), 1.0, 0.0), 0.0))
    bias_t = jnp.where(keep > 0.5, 0.0, -MASK_BIG)
    sigma = jnp.full((8, seq), slope * float(HEAD_DIM ** 0.5), F32)
    s1, s2, s3 = [p.astype(F32) for p in _split3(sigma)]
    srow = lax.broadcasted_iota(jnp.int32, (8, seq), 0)
    sig_rows = jnp.where((srow == 0) | (srow == 3), s1,
                         jnp.where((srow == 1) | (srow == 4), s2,
                                   jnp.where((srow == 2) | (srow == 5), s3, 0.0)))
    bias_pad = jnp.concatenate([bias_t, sig_rows, jnp.zeros((HEAD_DIM - 24, seq), F32)], axis=0)
    bias_nat = bias_pad.T

    qa_ref[:, 0:HEAD_DIM] = qb16
    qa_ref[:, HEAD_DIM:2 * HEAD_DIM] = bias_nat.astype(BF16)
    kpos = lax.broadcasted_iota(jnp.int32, (seq, HEAD_DIM), 0)
    krow = lax.shift_right_logical(kpos, MOBA_BLOCK_LOG2)
    klane = lax.broadcasted_iota(jnp.int32, (seq, HEAD_DIM), 1)
    pos_lo = (kpos & (MOBA_BLOCK - 1)).astype(F32)
    pos_hi = (kpos - (kpos & (MOBA_BLOCK - 1))).astype(F32)
    kfeat = jnp.where(klane == krow, 1.0,
                      jnp.where((klane >= 16) & (klane < 19), pos_lo,
                                jnp.where((klane >= 19) & (klane < 22), pos_hi, 0.0)))
    ka_ref[:, 0:HEAD_DIM] = kb16
    ka_ref[:, HEAD_DIM:2 * HEAD_DIM] = kfeat.astype(BF16)
    _store_vt(vt_ref, v)
    _attn_rows(qa_ref, ka_ref, vt_ref, o_ref)


def _fox_kernel(q_ref, k_ref, v_ref, gq_ref, gk_ref, fp_ref, wsrc_ref, w2src_ref, o_ref, wdst_ref, w2dst_ref,
                *scratch):
    wdst_ref[...] = wsrc_ref[...].astype(BF16)
    w2dst_ref[...] = w2src_ref[...].astype(BF16)
    for hh in range(HEADS_PER_STEP):
        cols = slice(hh * HEAD_DIM, (hh + 1) * HEAD_DIM)
        _fox_head(pl.program_id(1) * HEADS_PER_STEP + hh, q_ref[:, cols], k_ref[:, cols], v_ref[:, cols],
                  gq_ref, gk_ref, fp_ref, o_ref.at[:, cols], *scratch[3 * hh:3 * hh + 3])


def _fox_head(h, q, k, v, gq_ref, gk_ref, fp_ref, o_ref, qa_ref, ka_ref, vt_ref):
    qn = _rms(q, gq_ref[...])
    kn = _rms(k, gk_ref[...])
    packed = fp_ref[0]
    r = lax.broadcasted_iota(jnp.int32, (128, 128), 0)
    c = lax.broadcasted_iota(jnp.int32, (128, 128), 1)
    ones_row = r == 3 * FG_GROUP
    part = jnp.where(r == h, 0, jnp.where(r == FG_GROUP + h, 1, jnp.where(r == 2 * FG_GROUP + h, 2, -1)))
    pq = jnp.where(part == c, 1.0, jnp.where(ones_row & (c >= 3) & (c < 6), 1.0, 0.0)).astype(BF16)
    pk = jnp.where((part + 3 == c) & (part >= 0), -1.0, jnp.where(ones_row & (c < 3), 1.0, 0.0)).astype(BF16)
    qa_ref[:, 0:HEAD_DIM] = qn.astype(BF16)
    qa_ref[:, HEAD_DIM:2 * HEAD_DIM] = _dot(packed, pq).astype(BF16)
    ka_ref[:, 0:HEAD_DIM] = kn.astype(BF16)
    ka_ref[:, HEAD_DIM:2 * HEAD_DIM] = _dot(packed, pk).astype(BF16)
    _store_vt(vt_ref, v)
    _attn_rows(qa_ref, ka_ref, vt_ref, o_ref)


def _attn_scratch(seq):
    return HEADS_PER_STEP * [
        pltpu.VMEM((seq, 2 * HEAD_DIM), BF16),
        pltpu.VMEM((seq, 2 * HEAD_DIM), BF16),
        pltpu.VMEM((HEAD_DIM + VT_PAD, seq), BF16),
    ]


def _cast_slab_specs(w, n_steps, step_of):
    rows, cols = w.shape
    slab = rows // n_steps
    assert slab * n_steps == rows and slab % 16 == 0
    spec = pl.BlockSpec((slab, cols), lambda *g: (step_of(*g), 0))
    return spec, spec, jax.ShapeDtypeStruct((rows, cols), BF16)


def _moba(proj, slopes, gq, gk, batch, seq, n_heads, col0, w_f32):
    hps = HEADS_PER_STEP
    assert n_heads % hps == 0 and col0 % hps == 0
    n_hp = n_heads // hps
    hspec = lambda off: pl.BlockSpec((seq, hps * HEAD_DIM), lambda b, h: (b, (col0 + off) // hps + h))
    w_in_spec, w_out_spec, w_shape = _cast_slab_specs(w_f32, n_hp * batch, lambda b, h: b * n_hp + h)
    return pl.pallas_call(
        _moba_kernel,
        grid=(batch, n_hp),
        in_specs=[
            pl.BlockSpec(memory_space=pltpu.SMEM),
            hspec(0), hspec(n_heads), hspec(2 * n_heads),
            pl.BlockSpec((1, HEAD_DIM), lambda b, h: (0, 0)),
            pl.BlockSpec((1, HEAD_DIM), lambda b, h: (0, 0)),
            w_in_spec,
        ],
        out_specs=[pl.BlockSpec((seq, hps * HEAD_DIM), lambda b, h: (b, h)), w_out_spec],
        out_shape=[jax.ShapeDtypeStruct((batch * seq, n_heads * HEAD_DIM), BF16), w_shape],
        scratch_shapes=_attn_scratch(seq),
        compiler_params=_cparams(2, 56),
        name="moba_attention",
    )(slopes, proj, proj, proj, gq, gk, w_f32)


def _fox(proj, fpack, gq, gk, batch, seq, n_heads, col0, w_f32, w2_f32):
    hps = HEADS_PER_STEP
    assert n_heads % hps == 0 and col0 % hps == 0
    n_hp = n_heads // hps
    hspec = lambda off: pl.BlockSpec((seq, hps * HEAD_DIM), lambda b, h: (b, (col0 + off) // hps + h))
    step_of = lambda b, h: b * n_hp + h
    w_in_spec, w_out_spec, w_shape = _cast_slab_specs(w_f32, n_hp * batch, step_of)
    w2_in_spec, w2_out_spec, w2_shape = _cast_slab_specs(w2_f32, n_hp * batch, step_of)
    return pl.pallas_call(
        _fox_kernel,
        grid=(batch, n_hp),
        in_specs=[
            hspec(0), hspec(n_heads), hspec(2 * n_heads),
            pl.BlockSpec((1, HEAD_DIM), lambda b, h: (0, 0)),
            pl.BlockSpec((1, HEAD_DIM), lambda b, h: (0, 0)),
            pl.BlockSpec((1, seq, 128), lambda b, h: (b, 0, 0)),
            w_in_spec, w2_in_spec,
        ],
        out_specs=[pl.BlockSpec((seq, hps * HEAD_DIM), lambda b, h: (b, h)), w_out_spec, w2_out_spec],
        out_shape=[jax.ShapeDtypeStruct((batch * seq, n_heads * HEAD_DIM), BF16), w_shape, w2_shape],
        scratch_shapes=_attn_scratch(seq),
        compiler_params=_cparams(2, 56),
        name="fox_attention",
    )(proj, proj, proj, gq, gk, fpack, w_f32, w2_f32)


def _top16_sorted(st, want_rank=False):
    tt = st.shape[1]
    rid = lax.broadcasted_iota(jnp.int32, (PEER_TOPK, tt), 0)
    sv = jnp.zeros((PEER_TOPK, tt), F32)
    rank = jnp.full(st.shape, float(PEER_TOPK), F32) if want_rank else None
    work = st
    for r in range(PEER_TOPK):
        mx = jnp.max(work, axis=0, keepdims=True)
        sv = jnp.where(rid == r, mx, sv)
        hit = work == mx
        if want_rank:
            rank = jnp.where(hit, float(r), rank)
        if r + 1 < PEER_TOPK:
            work = jnp.where(hit, -jnp.inf, work)
    return (sv, rank) if want_rank else sv


def _peer_select_kernel(qp_ref, keys_ref, w_ref, r2_s, e2_s, k1_s, c1_s):
    tt = qp_ref.shape[0]
    n_lc = tt // 128
    b8 = lax.broadcasted_iota(jnp.int32, (8, tt), 0)
    inf = jnp.inf
    for h in range(PEER_HEADS):
        q1 = qp_ref[:, (2 * h) * PEER_HALF:(2 * h + 1) * PEER_HALF].astype(BF16)
        q2 = qp_ref[:, (2 * h + 1) * PEER_HALF:(2 * h + 2) * PEER_HALF].astype(BF16)
        s1 = _dot_nt(keys_ref[2 * h], q1)
        s2 = _dot_nt(keys_ref[2 * h + 1], q2)
        sv0 = _top16_sorted(s1)
        sv1, rank2 = _top16_sorted(s2, want_rank=True)
        sv1a, sv1b = sv1[0:8, :], sv1[8:16, :]
        row = lambda a: sv0[a:a + 1, :]
        cands = [(row(0) + sv1a, sv1a), (row(0) + sv1b, sv1b), (row(1) + sv1a, sv1a)]
        for a, nbv in ((2, 5), (3, 4), (4, 3), (5, 2), (6, 2), (7, 2)):
            cands.append((jnp.where(b8 < nbv, row(a) + sv1a, -inf), sv1a))
        top1 = sv1[0:1, :]
        hsum = sv0[8:16, :] + top1
        work = [g for g, _ in cands] + [hsum]
        tau = None
        for r in range(PEER_TOPK):
            mx = functools.reduce(jnp.maximum, work)
            tau = jnp.max(mx, axis=0, keepdims=True)
            if r + 1 < PEER_TOPK:
                work = [jnp.where(g == tau, -inf, g) for g in work]
        top = row(0) + top1
        zsum = jnp.zeros((1, tt), F32)
        for g in [g for g, _ in cands] + [hsum]:
            zsum = zsum + jnp.sum(jnp.where(g >= tau, jnp.exp(g - top), 0.0), axis=0, keepdims=True)
        inv_z = 1.0 / zsum
        cnt = lambda g: jnp.sum(jnp.where(g >= tau, 1.0, 0.0), axis=0, keepdims=True)
        k_rows = [cnt(cands[0][0]) + cnt(cands[1][0])] + [cnt(g) for g, _ in cands[2:]]
        k_hi = jnp.where(hsum >= tau, 1.0, 0.0)
        k1 = jnp.zeros(s1.shape, F32)
        for a in range(PEER_TOPK):
            ka = k_rows[a] if a < 8 else k_hi[a - 8:a - 7, :]
            k1 = jnp.where(s1 == row(a), ka, k1)
        e2 = jnp.exp(s2 - top1)
        c1 = jnp.exp(s1 - row(0)) * inv_z
        for lc in range(n_lc):
            cols = slice(lc * 128, (lc + 1) * 128)
            r2_s[h, lc] = rank2[:, cols].astype(BF16)
            e2_s[h, lc] = e2[:, cols].astype(BF16)
            k1_s[h, lc] = k1[:, cols]
            c1_s[h, lc] = c1[:, cols]

    def gate_rows(i, carry):
        for lc in range(n_lc):
            acc = None
            for h in range(PEER_HEADS):
                k1 = jnp.broadcast_to(k1_s[h, lc, pl.ds(i, 1), :], (16, 128)).astype(BF16)
                c1 = jnp.broadcast_to(c1_s[h, lc, pl.ds(i, 1), :], (16, 128)).astype(BF16)
                k1 = jnp.tile(k1, (PEER_NKEYS // 16, 1))
                c1 = jnp.tile(c1, (PEER_NKEYS // 16, 1))
                w = jnp.where(r2_s[h, lc] < k1, e2_s[h, lc] * c1, jnp.zeros_like(c1))
                acc = w if acc is None else acc + w
            w_ref[pl.ds(pl.multiple_of(i * PEER_NKEYS, PEER_NKEYS), PEER_NKEYS), lc * 128:(lc + 1) * 128] = acc
        return carry

    lax.fori_loop(0, PEER_NKEYS, gate_rows, 0, unroll=2)


def _peer_select(qp, keys16):
    t = qp.shape[0]
    tt = 256
    n_exp = PEER_NKEYS * PEER_NKEYS
    stage = pltpu.VMEM((PEER_HEADS, tt // 128, PEER_NKEYS, 128), F32)
    stage16 = pltpu.VMEM((PEER_HEADS, tt // 128, PEER_NKEYS, 128), BF16)
    return pl.pallas_call(
        _peer_select_kernel,
        grid=(t // tt,),
        in_specs=[
            pl.BlockSpec((tt, qp.shape[1]), lambda i: (i, 0)),
            pl.BlockSpec(keys16.shape, lambda i: (0, 0, 0)),
        ],
        out_specs=pl.BlockSpec((n_exp, tt), lambda i: (0, i)),
        out_shape=jax.ShapeDtypeStruct((n_exp, t), BF16),
        scratch_shapes=[stage16, stage16, stage, stage],
        compiler_params=_cparams(1, 48),
        name="peer_select",
    )(qp, keys16)


def _peer_dense_kernel(h_ref, u_ref, v_ref, w_ref, x_ref, mod_ref, o_ref):
    e = pl.program_id(1)

    @pl.when(e == 0)
    def _():
        o_ref[...] = jnp.zeros_like(o_ref)

    at = _dot_nt(u_ref[...], h_ref[...])
    gelu = 0.5 * at * (1.0 + lax.erf(at * float(np.sqrt(0.5))))
    wt = (w_ref[...].astype(F32) * gelu).astype(BF16)
    o_ref[...] += _dot_tn(wt, v_ref[...])

    @pl.when(e == pl.num_programs(1) - 1)
    def _():
        o_ref[...] = x_ref[...] + mod_ref[0, 5:6, :] * o_ref[...]


def _peer_dense(h2, u16, v16, wgt, x1, mod3, seq):
    t, d = h2.shape
    n_exp = u16.shape[0]
    tm = 512 if t % 512 == 0 else t
    tm = min(tm, seq)
    te = 512
    return pl.pallas_call(
        _peer_dense_kernel,
        grid=(t // tm, n_exp // te),
        in_specs=[
            pl.BlockSpec((tm, d), lambda i, e: (i, 0)),
            pl.BlockSpec((te, d), lambda i, e: (e, 0)),
            pl.BlockSpec((te, d), lambda i, e: (e, 0)),
            pl.BlockSpec((te, tm), lambda i, e: (e, i)),
            pl.BlockSpec((tm, d), lambda i, e: (i, 0)),
            pl.BlockSpec((1, N_MOD, d), lambda i, e: ((i * tm) // seq, 0, 0)),
        ],
        out_specs=pl.BlockSpec((tm, d), lambda i, e: (i, 0)),
        out_shape=jax.ShapeDtypeStruct((t, d), F32),
        compiler_params=_cparams(2, 60),
        name="peer_dense",
    )(h2, u16, v16, wgt, x1, mod3)


def _layer(x, c8, w_ada, b_ada, norm1_g, w_in, b_f, qn_m, kn_m, qn_f, kn_f, w_out, norm2_g,
           w_pq, sub_keys, peer_u, peer_v):
    b, s, d = x.shape
    t = b * s
    n_heads = d // HEAD_DIM
    n_moba = n_heads // 2
    n_fox = n_heads - n_moba
    d_moba = n_moba * HEAD_DIM
    d_fox = n_fox * HEAD_DIM
    d_qkv = 3 * d_moba + 3 * d_fox
    assert s % ATTN_TILE == 0 and ATTN_TILE % MOBA_BLOCK == 0 and s % 512 == 0
    assert n_fox <= FG_GROUP and s // MOBA_BLOCK <= 16 and b <= 8

    mod = _adaln(c8, w_ada, b_ada.reshape(1, -1))
    mod3 = mod[:b].reshape(b, N_MOD, d)

    w_in_t = jnp.swapaxes(w_in, 0, 1)
    w_fg_t = w_in_t[d_qkv:]
    wfg3 = jnp.zeros((128, d), F32)
    bfg3 = jnp.zeros((1, 128), F32)
    for g in range(3):
        wfg3 = wfg3.at[g * FG_GROUP:g * FG_GROUP + n_fox, :].set(w_fg_t)
        bfg3 = bfg3.at[0, g * FG_GROUP:g * FG_GROUP + n_fox].set(b_f)
    h1, fpack = _norm1(x, norm1_g.reshape(1, d), mod3, wfg3.astype(BF16), bfg3, n_fox)

    proj = _matmul(h1.reshape(t, d), w_in_t, d_qkv, F32, "in_proj", w_rows_are_outputs=True)

    slopes = 2.0 ** (-8.0 * jnp.arange(1, n_moba + 1, dtype=F32) / n_moba)
    o_moba, u16 = _moba(proj, slopes, qn_m.reshape(1, -1), kn_m.reshape(1, -1), b, s, n_moba, 0, peer_u)
    o_fox, v16, wq16 = _fox(proj, fpack, qn_f.reshape(1, -1), kn_f.reshape(1, -1), b, s, n_fox, 3 * n_moba,
                            peer_v, w_pq)

    x1 = _outproj(o_moba, o_fox, w_out, x.reshape(t, d), mod3, s).reshape(b, s, d)

    h2, qp = _norm2_query(x1, norm2_g.reshape(1, d), mod3, wq16)
    h2 = h2.reshape(t, d)
    keys16 = sub_keys.reshape(2 * PEER_HEADS, PEER_NKEYS, PEER_HALF).astype(BF16)
    wgt = _peer_select(qp, keys16)
    out = _peer_dense(h2, u16, v16, wgt, x1.reshape(t, d), mod3, s)
    return out.reshape(b, s, d)


def kernel(x, c, w_ada, b_ada, norm1_g, w_in, b_f, q_norm_moba, k_norm_moba, q_norm_fox, k_norm_fox,
           w_out, norm2_g, w_pq, peer_sub_keys, peer_u, peer_v):
    b = x.shape[0]
    c8 = jnp.zeros((8, c.shape[1]), F32).at[:b].set(c)
    for l in range(w_ada.shape[0]):
        x = _layer(x, c8, w_ada[l], b_ada[l], norm1_g[l], w_in[l], b_f[l], q_norm_moba[l], k_norm_moba[l],
                   q_norm_fox[l], k_norm_fox[l], w_out[l], norm2_g[l], w_pq[l], peer_sub_keys[l],
                   peer_u[l], peer_v[l])
    return x
```

```python
import functools

import jax
import jax.numpy as jnp
import numpy as np
from jax import lax
from jax.experimental import pallas as pl
from jax.experimental.pallas import tpu as pltpu

F32 = jnp.float32
BF16 = jnp.bfloat16

HEAD_DIM = 128
MOBA_BLOCK = 256
MOBA_BLOCK_LOG2 = 8
MOBA_TOPK = 3
PEER_HEADS = 8
PEER_NKEYS = 128
PEER_TOPK = 16
PEER_HALF = 128
N_MOD = 6
NORM_EPS = 1e-6
NEG_INF = -1e30
MASK_BIG = 2.0 ** 100
ATTN_TILE = 512
LOG2E = 1.4426950408889634
FG_GROUP = 16
VT_PAD = 16
HEADS_PER_STEP = 2

MIB = 1024 * 1024


def _cparams(n_axes, vmem_mib):
    return pltpu.CompilerParams(
        dimension_semantics=("arbitrary",) * n_axes,
        vmem_limit_bytes=int(vmem_mib * MIB),
    )


def _dot(a, b):
    return jnp.dot(a, b, preferred_element_type=F32)


def _dot_nt(a, b):
    return lax.dot_general(a, b, (((1,), (1,)), ((), ())), preferred_element_type=F32)


def _dot_tn(a, b):
    return lax.dot_general(a, b, (((0,), (0,)), ((), ())), preferred_element_type=F32)


def _split3(x):
    p1 = x.astype(BF16)
    r1 = x - p1.astype(F32)
    p2 = r1.astype(BF16)
    r2 = r1 - p2.astype(F32)
    p3 = r2.astype(BF16)
    return p1, p2, p3


def _rms(x, g):
    ms = jnp.mean(x * x, axis=-1, keepdims=True)
    return x * lax.rsqrt(ms + NORM_EPS) * g


def _adaln_kernel(c_ref, w_ref, b_ref, o_ref):
    c = c_ref[...]
    ca = c * (1.0 / (1.0 + jnp.exp(-c)))
    o_ref[...] = _dot(ca.astype(BF16), w_ref[...].astype(BF16)) + b_ref[...]


def _adaln(c_pad, w, b_row):
    rows, d = c_pad.shape
    n = w.shape[1]
    tn = 512
    return pl.pallas_call(
        _adaln_kernel,
        grid=(n // tn,),
        in_specs=[
            pl.BlockSpec((rows, d), lambda j: (0, 0)),
            pl.BlockSpec((d, tn), lambda j: (0, j)),
            pl.BlockSpec((1, tn), lambda j: (0, j)),
        ],
        out_specs=pl.BlockSpec((rows, tn), lambda j: (0, j)),
        out_shape=jax.ShapeDtypeStruct((rows, n), F32),
        compiler_params=_cparams(1, 40),
        name="adaln",
    )(c_pad, w, b_row)


def _norm_mod(x_ref, g_ref, mod_ref, row0):
    xf = x_ref[0]
    y = _rms(xf, g_ref[...])
    sh = mod_ref[0, row0:row0 + 1, :]
    sc = mod_ref[0, row0 + 1:row0 + 2, :]
    return y * (1.0 + sc) + sh


def _norm1_kernel(x_ref, g_ref, mod_ref, wfg_ref, bfg_ref, h_ref, fp_ref, carry_ref, *, inv_scale, n_fox):
    s_idx = pl.program_id(1)
    hb = _norm_mod(x_ref, g_ref, mod_ref, 0).astype(BF16)
    h_ref[0] = hb
    ts = hb.shape[0]

    @pl.when(s_idx == 0)
    def _():
        carry_ref[...] = jnp.zeros_like(carry_ref)

    z = _dot_nt(hb, wfg_ref[...]) + bfg_ref[...]
    lf = jnp.minimum(z, 0.0) - jnp.log1p(jnp.exp(-jnp.abs(z)))
    r = lax.broadcasted_iota(jnp.int32, (ts, ts), 0)
    c = lax.broadcasted_iota(jnp.int32, (ts, ts), 1)
    tri = jnp.where(c <= r, 1.0, 0.0).astype(BF16)
    l1, l2, l3 = _split3(lf)
    f = (_dot(tri, l3) + _dot(tri, l2)) + _dot(tri, l1) + carry_ref[...]
    carry_ref[...] = f[ts - 1:ts, :]
    f1, f2, f3 = [p.astype(F32) for p in _split3(f * inv_scale)]
    lane = lax.broadcasted_iota(jnp.int32, f.shape, 1)
    packed = jnp.where(lane < FG_GROUP, f1,
              jnp.where(lane < 2 * FG_GROUP, f2,
               jnp.where(lane < 3 * FG_GROUP, f3,
                jnp.where(lane == 3 * FG_GROUP, 1.0, 0.0))))
    fp_ref[0] = packed.astype(BF16)


def _norm1(x, g_row, mod3, wfg3, bfg3, n_fox):
    b, s, d = x.shape
    ts = 512
    kern = functools.partial(_norm1_kernel, inv_scale=float(HEAD_DIM ** 0.5), n_fox=n_fox)
    return pl.pallas_call(
        kern,
        grid=(b, s // ts),
        in_specs=[
            pl.BlockSpec((1, ts, d), lambda i, j: (i, j, 0)),
            pl.BlockSpec((1, d), lambda i, j: (0, 0)),
            pl.BlockSpec((1, N_MOD, d), lambda i, j: (i, 0, 0)),
            pl.BlockSpec((128, d), lambda i, j: (0, 0)),
            pl.BlockSpec((1, 128), lambda i, j: (0, 0)),
        ],
        out_specs=[
            pl.BlockSpec((1, ts, d), lambda i, j: (i, j, 0)),
            pl.BlockSpec((1, ts, 128), lambda i, j: (i, j, 0)),
        ],
        out_shape=[
            jax.ShapeDtypeStruct((b, s, d), BF16),
            jax.ShapeDtypeStruct((b, s, 128), BF16),
        ],
        scratch_shapes=[pltpu.VMEM((1, 128), F32)],
        compiler_params=_cparams(2, 48),
        name="norm1_fgate",
    )(x, g_row, mod3, wfg3, bfg3)


def _norm2_kernel(x_ref, g_ref, mod_ref, w_ref, h_ref, q_ref):
    hb = _norm_mod(x_ref, g_ref, mod_ref, 3).astype(BF16)
    h_ref[0] = hb
    q_ref[...] = _dot(hb, w_ref[...])


def _norm2_query(x, g_row, mod3, wq16):
    b, s, d = x.shape
    nq = wq16.shape[1]
    ts = 512
    nj = s // ts
    return pl.pallas_call(
        _norm2_kernel,
        grid=(b, nj),
        in_specs=[
            pl.BlockSpec((1, ts, d), lambda i, j: (i, j, 0)),
            pl.BlockSpec((1, d), lambda i, j: (0, 0)),
            pl.BlockSpec((1, N_MOD, d), lambda i, j: (i, 0, 0)),
            pl.BlockSpec((d, nq), lambda i, j: (0, 0), pipeline_mode=pl.Buffered(1)),
        ],
        out_specs=[
            pl.BlockSpec((1, ts, d), lambda i, j: (i, j, 0)),
            pl.BlockSpec((ts, nq), lambda i, j: (i * nj + j, 0)),
        ],
        out_shape=[
            jax.ShapeDtypeStruct((b, s, d), BF16),
            jax.ShapeDtypeStruct((b * s, nq), F32),
        ],
        compiler_params=_cparams(2, 58),
        name="norm2_peer_query",
    )(x, g_row, mod3, wq16)


MM_TM = 1024
MM_TN = 512


def _cast_weight(w_ref, wb_ref):
    @pl.when(pl.program_id(1) == 0)
    def _():
        wb_ref[...] = w_ref[...].astype(BF16)


def _mm_kernel(a_ref, w_ref, o_ref, wb_ref, *, w_rows_are_outputs):
    _cast_weight(w_ref, wb_ref)
    dot = _dot_nt if w_rows_are_outputs else _dot
    o_ref[...] = dot(a_ref[...], wb_ref[...]).astype(o_ref.dtype)


def _matmul(a, w, n, out_dtype, name, w_rows_are_outputs=False):
    m, k = a.shape
    tm = MM_TM if m % MM_TM == 0 else m
    tn = MM_TN
    assert n % tn == 0
    if w_rows_are_outputs:
        w_block, w_spec = (tn, k), pl.BlockSpec((tn, k), lambda j, i: (j, 0))
    else:
        w_block, w_spec = (k, tn), pl.BlockSpec((k, tn), lambda j, i: (0, j))
    return pl.pallas_call(
        functools.partial(_mm_kernel, w_rows_are_outputs=w_rows_are_outputs),
        grid=(n // tn, m // tm),
        in_specs=[pl.BlockSpec((tm, k), lambda j, i: (i, 0)), w_spec],
        out_specs=pl.BlockSpec((tm, tn), lambda j, i: (i, j)),
        out_shape=jax.ShapeDtypeStruct((m, n), out_dtype),
        scratch_shapes=[pltpu.VMEM(w_block, BF16)],
        compiler_params=_cparams(2, 48),
        name=name,
    )(a, w)


def _outproj_kernel(oa_ref, ob_ref, w_ref, x_ref, mod_ref, o_ref):
    ka = oa_ref.shape[1]
    acc = _dot(oa_ref[...], w_ref[0:ka, :]) + _dot(ob_ref[...], w_ref[ka:, :])
    gate = mod_ref[0, 2:3, :]
    o_ref[...] = x_ref[...] + gate * acc


def _outproj(oa, ob, w16, x2d, mod3, seq):
    m, ka = oa.shape
    kb = ob.shape[1]
    k, n = w16.shape
    assert ka + kb == k
    tm = min(MM_TM if m % MM_TM == 0 else m, seq)
    tn = 1024 if n % 1024 == 0 else MM_TN
    return pl.pallas_call(
        _outproj_kernel,
        grid=(m // tm, n // tn),
        in_specs=[
            pl.BlockSpec((tm, ka), lambda i, j: (i, 0)),
            pl.BlockSpec((tm, kb), lambda i, j: (i, 0)),
            pl.BlockSpec((k, tn), lambda i, j: (0, j)),
            pl.BlockSpec((tm, tn), lambda i, j: (i, j)),
            pl.BlockSpec((1, N_MOD, tn), lambda i, j: ((i * tm) // seq, 0, j)),
        ],
        out_specs=pl.BlockSpec((tm, tn), lambda i, j: (i, j)),
        out_shape=jax.ShapeDtypeStruct((m, n), F32),
        compiler_params=_cparams(2, 56),
        name="out_proj_residual",
    )(oa, ob, w16, x2d, mod3)


def _store_vt(vt_ref, v):
    seq = v.shape[0]
    vt_ref[0:HEAD_DIM, :] = v.T.astype(BF16)
    rid = lax.broadcasted_iota(jnp.int32, (VT_PAD, seq), 0)
    vt_ref[HEAD_DIM:HEAD_DIM + VT_PAD, :] = jnp.where(rid == 0, 1.0, 0.0).astype(BF16)


def _attn_rows(qa_ref, ka_ref, vt_ref, o_ref):
    seq = qa_ref.shape[0]
    t = ATTN_TILE
    c_log2 = float(HEAD_DIM ** -0.5 * LOG2E)
    row = lax.broadcasted_iota(jnp.int32, (t, t), 0)
    col = lax.broadcasted_iota(jnp.int32, (t, t), 1)
    causal = row <= col
    for qp in range(seq // t):
        kv = (qp + 1) * t
        qa = qa_ref[qp * t:(qp + 1) * t, :]
        st = _dot_nt(ka_ref[0:kv, :], qa) * c_log2
        tail = jnp.where(causal, st[kv - t:kv, :], NEG_INF)
        m = jnp.max(tail, axis=0, keepdims=True)
        if qp > 0:
            head = st[0:kv - t, :]
            m = jnp.maximum(m, jnp.max(head, axis=0, keepdims=True))
            p_head = jnp.exp2(head - m)
        p_tail = jnp.exp2(tail - m)
        acc = _dot(vt_ref[:, kv - t:kv], p_tail.astype(BF16))
        if qp > 0:
            acc = acc + _dot(vt_ref[:, 0:kv - t], p_head.astype(BF16))
        ot = acc[0:HEAD_DIM, :] * (1.0 / acc[HEAD_DIM:HEAD_DIM + 1, :])
        o_ref[qp * t:(qp + 1) * t, :] = ot.T.astype(o_ref.dtype)


def _moba_kernel(slopes_ref, q_ref, k_ref, v_ref, gq_ref, gk_ref, wsrc_ref, w2src_ref, o_ref, wdst_ref, w2dst_ref,
                 *scratch):
    wdst_ref[...] = wsrc_ref[...].astype(BF16)
    w2dst_ref[...] = w2src_ref[...].astype(BF16)
    for hh in range(HEADS_PER_STEP):
        cols = slice(hh * HEAD_DIM, (hh + 1) * HEAD_DIM)
        _moba_head(slopes_ref[pl.program_id(1) * HEADS_PER_STEP + hh], q_ref[:, cols], k_ref[:, cols], v_ref[:, cols],
                   gq_ref, gk_ref, o_ref.at[:, cols], *scratch[3 * hh:3 * hh + 3])


def _moba_head(slope, q, k, v, gq_ref, gk_ref, o_ref, qa_ref, ka_ref, vt_ref):
    seq = q.shape[0]
    nb = seq // MOBA_BLOCK
    qn = _rms(q, gq_ref[...])
    kn = _rms(k, gk_ref[...])
    qb16 = qn.astype(BF16)
    kb16 = kn.astype(BF16)

    rid = lax.broadcasted_iota(jnp.int32, (16, HEAD_DIM), 0)
    kmean = jnp.zeros((16, HEAD_DIM), F32)
    for n in range(nb):
        blk = jnp.sum(kn[n * MOBA_BLOCK:(n + 1) * MOBA_BLOCK, :], axis=0, keepdims=True) / float(MOBA_BLOCK)
        kmean = jnp.where(rid == n, blk, kmean)
    gt = _dot_nt(kmean.astype(BF16), qb16)

    bidx = lax.broadcasted_iota(jnp.int32, (16, seq), 0)
    qblk = lax.shift_right_logical(lax.broadcasted_iota(jnp.int32, (16, seq), 1), MOBA_BLOCK_LOG2)
    rank = jnp.zeros((16, seq), F32)
    for n in range(nb):
        gn = gt[n:n + 1, :]
        beats = jnp.where(gn > gt, 1.0, jnp.where(gn == gt, jnp.where(bidx > n, 1.0, 0.0), 0.0))
        rank = rank + jnp.where(qblk > n, beats, 0.0)
    keep = jnp.where(bidx == qblk, 1.0,
                     jnp.where(bidx < qblk, jnp.where(rank < float(MOBA_TOPK), 1.0, 0.0), 0.0))
    bias_t = jnp.where(keep > 0.5, 0.0, -MASK_BIG)
    sigma = jnp.full((8, seq), slope * float(HEAD_DIM ** 0.5), F32)
    s1, s2, s3 = [p.astype(F32) for p in _split3(sigma)]
    srow = lax.broadcasted_iota(jnp.int32, (8, seq), 0)
    sig_rows = jnp.where((srow == 0) | (srow == 3), s1,
                         jnp.where((srow == 1) | (srow == 4), s2,
                                   jnp.where((srow == 2) | (srow == 5), s3, 0.0)))
    bias_pad = jnp.concatenate([bias_t, sig_rows, jnp.zeros((HEAD_DIM - 24, seq), F32)], axis=0)
    bias_nat = bias_pad.T

    qa_ref[:, 0:HEAD_DIM] = qb16
    qa_ref[:, HEAD_DIM:2 * HEAD_DIM] = bias_nat.astype(BF16)
    kpos = lax.broadcasted_iota(jnp.int32, (seq, HEAD_DIM), 0)
    krow = lax.shift_right_logical(kpos, MOBA_BLOCK_LOG2)
    klane = lax.broadcasted_iota(jnp.int32, (seq, HEAD_DIM), 1)
    pos_lo = (kpos & (MOBA_BLOCK - 1)).astype(F32)
    pos_hi = (kpos - (kpos & (MOBA_BLOCK - 1))).astype(F32)
    kfeat = jnp.where(klane == krow, 1.0,
                      jnp.where((klane >= 16) & (klane < 19), pos_lo,
                                jnp.where((klane >= 19) & (klane < 22), pos_hi, 0.0)))
    ka_ref[:, 0:HEAD_DIM] = kb16
    ka_ref[:, HEAD_DIM:2 * HEAD_DIM] = kfeat.astype(BF16)
    _store_vt(vt_ref, v)
    _attn_rows(qa_ref, ka_ref, vt_ref, o_ref)


def _fox_kernel(q_ref, k_ref, v_ref, gq_ref, gk_ref, fp_ref, wsrc_ref, w2src_ref, o_ref, wdst_ref, w2dst_ref,
                *scratch):
    wdst_ref[...] = wsrc_ref[...].astype(BF16)
    w2dst_ref[...] = w2src_ref[...].astype(BF16)
    for hh in range(HEADS_PER_STEP):
        cols = slice(hh * HEAD_DIM, (hh + 1) * HEAD_DIM)
        _fox_head(pl.program_id(1) * HEADS_PER_STEP + hh, q_ref[:, cols], k_ref[:, cols], v_ref[:, cols],
                  gq_ref, gk_ref, fp_ref, o_ref.at[:, cols], *scratch[3 * hh:3 * hh + 3])


def _fox_head(h, q, k, v, gq_ref, gk_ref, fp_ref, o_ref, qa_ref, ka_ref, vt_ref):
    qn = _rms(q, gq_ref[...])
    kn = _rms(k, gk_ref[...])
    packed = fp_ref[0]
    r = lax.broadcasted_iota(jnp.int32, (128, 128), 0)
    c = lax.broadcasted_iota(jnp.int32, (128, 128), 1)
    ones_row = r == 3 * FG_GROUP
    part = jnp.where(r == h, 0, jnp.where(r == FG_GROUP + h, 1, jnp.where(r == 2 * FG_GROUP + h, 2, -1)))
    pq = jnp.where(part == c, 1.0, jnp.where(ones_row & (c >= 3) & (c < 6), 1.0, 0.0)).astype(BF16)
    pk = jnp.where((part + 3 == c) & (part >= 0), -1.0, jnp.where(ones_row & (c < 3), 1.0, 0.0)).astype(BF16)
    qa_ref[:, 0:HEAD_DIM] = qn.astype(BF16)
    qa_ref[:, HEAD_DIM:2 * HEAD_DIM] = _dot(packed, pq).astype(BF16)
    ka_ref[:, 0:HEAD_DIM] = kn.astype(BF16)
    ka_ref[:, HEAD_DIM:2 * HEAD_DIM] = _dot(packed, pk).astype(BF16)
    _store_vt(vt_ref, v)
    _attn_rows(qa_ref, ka_ref, vt_ref, o_ref)


def _attn_scratch(seq):
    return HEADS_PER_STEP * [
        pltpu.VMEM((seq, 2 * HEAD_DIM), BF16),
        pltpu.VMEM((seq, 2 * HEAD_DIM), BF16),
        pltpu.VMEM((HEAD_DIM + VT_PAD, seq), BF16),
    ]


def _cast_slab_specs(w, n_steps, step_of):
    rows, cols = w.shape
    slab = rows // n_steps
    assert slab * n_steps == rows and slab % 16 == 0
    spec = pl.BlockSpec((slab, cols), lambda *g: (step_of(*g), 0))
    return spec, spec, jax.ShapeDtypeStruct((rows, cols), BF16)


def _moba(proj, slopes, gq, gk, batch, seq, n_heads, col0, w_f32, w2_f32):
    hps = HEADS_PER_STEP
    assert n_heads % hps == 0 and col0 % hps == 0
    n_hp = n_heads // hps
    hspec = lambda off: pl.BlockSpec((seq, hps * HEAD_DIM), lambda b, h: (b, (col0 + off) // hps + h))
    step_of = lambda b, h: b * n_hp + h
    w_in_spec, w_out_spec, w_shape = _cast_slab_specs(w_f32, n_hp * batch, step_of)
    w2_in_spec, w2_out_spec, w2_shape = _cast_slab_specs(w2_f32, n_hp * batch, step_of)
    return pl.pallas_call(
        _moba_kernel,
        grid=(batch, n_hp),
        in_specs=[
            pl.BlockSpec(memory_space=pltpu.SMEM),
            hspec(0), hspec(n_heads), hspec(2 * n_heads),
            pl.BlockSpec((1, HEAD_DIM), lambda b, h: (0, 0)),
            pl.BlockSpec((1, HEAD_DIM), lambda b, h: (0, 0)),
            w_in_spec, w2_in_spec,
        ],
        out_specs=[pl.BlockSpec((seq, hps * HEAD_DIM), lambda b, h: (b, h)), w_out_spec, w2_out_spec],
        out_shape=[jax.ShapeDtypeStruct((batch * seq, n_heads * HEAD_DIM), BF16), w_shape, w2_shape],
        scratch_shapes=_attn_scratch(seq),
        compiler_params=_cparams(2, 58),
        name="moba_attention",
    )(slopes, proj, proj, proj, gq, gk, w_f32, w2_f32)


def _fox(proj, fpack, gq, gk, batch, seq, n_heads, col0, w_f32, w2_f32):
    hps = HEADS_PER_STEP
    assert n_heads % hps == 0 and col0 % hps == 0
    n_hp = n_heads // hps
    hspec = lambda off: pl.BlockSpec((seq, hps * HEAD_DIM), lambda b, h: (b, (col0 + off) // hps + h))
    step_of = lambda b, h: b * n_hp + h
    w_in_spec, w_out_spec, w_shape = _cast_slab_specs(w_f32, n_hp * batch, step_of)
    w2_in_spec, w2_out_spec, w2_shape = _cast_slab_specs(w2_f32, n_hp * batch, step_of)
    return pl.pallas_call(
        _fox_kernel,
        grid=(batch, n_hp),
        in_specs=[
            hspec(0), hspec(n_heads), hspec(2 * n_heads),
            pl.BlockSpec((1, HEAD_DIM), lambda b, h: (0, 0)),
            pl.BlockSpec((1, HEAD_DIM), lambda b, h: (0, 0)),
            pl.BlockSpec((1, seq, 128), lambda b, h: (b, 0, 0)),
            w_in_spec, w2_in_spec,
        ],
        out_specs=[pl.BlockSpec((seq, hps * HEAD_DIM), lambda b, h: (b, h)), w_out_spec, w2_out_spec],
        out_shape=[jax.ShapeDtypeStruct((batch * seq, n_heads * HEAD_DIM), BF16), w_shape, w2_shape],
        scratch_shapes=_attn_scratch(seq),
        compiler_params=_cparams(2, 56),
        name="fox_attention",
    )(proj, proj, proj, gq, gk, fpack, w_f32, w2_f32)


def _top16_sorted(st, want_rank=False):
    tt = st.shape[1]
    rid = lax.broadcasted_iota(jnp.int32, (PEER_TOPK, tt), 0)
    sv = jnp.zeros((PEER_TOPK, tt), F32)
    rank = jnp.full(st.shape, float(PEER_TOPK), F32) if want_rank else None
    work = st
    for r in range(PEER_TOPK):
        mx = jnp.max(work, axis=0, keepdims=True)
        sv = jnp.where(rid == r, mx, sv)
        hit = work == mx
        if want_rank:
            rank = jnp.where(hit, float(r), rank)
        if r + 1 < PEER_TOPK:
            work = jnp.where(hit, -jnp.inf, work)
    return (sv, rank) if want_rank else sv


def _peer_select_kernel(qp_ref, keys_ref, w_ref, r2_s, e2_s, k1_s, c1_s):
    tt = qp_ref.shape[0]
    n_lc = tt // 128
    b8 = lax.broadcasted_iota(jnp.int32, (8, tt), 0)
    inf = jnp.inf
    for h in range(PEER_HEADS):
        q1 = qp_ref[:, (2 * h) * PEER_HALF:(2 * h + 1) * PEER_HALF].astype(BF16)
        q2 = qp_ref[:, (2 * h + 1) * PEER_HALF:(2 * h + 2) * PEER_HALF].astype(BF16)
        s1 = _dot_nt(keys_ref[2 * h], q1)
        s2 = _dot_nt(keys_ref[2 * h + 1], q2)
        sv0 = _top16_sorted(s1)
        sv1, rank2 = _top16_sorted(s2, want_rank=True)
        sv1a, sv1b = sv1[0:8, :], sv1[8:16, :]
        row = lambda a: sv0[a:a + 1, :]
        cands = [(row(0) + sv1a, sv1a), (row(0) + sv1b, sv1b), (row(1) + sv1a, sv1a)]
        for a, nbv in ((2, 5), (3, 4), (4, 3), (5, 2), (6, 2), (7, 2)):
            cands.append((jnp.where(b8 < nbv, row(a) + sv1a, -inf), sv1a))
        top1 = sv1[0:1, :]
        hsum = sv0[8:16, :] + top1
        work = [g for g, _ in cands] + [hsum]
        tau = None
        for r in range(PEER_TOPK):
            mx = functools.reduce(jnp.maximum, work)
            tau = jnp.max(mx, axis=0, keepdims=True)
            if r + 1 < PEER_TOPK:
                work = [jnp.where(g == tau, -inf, g) for g in work]
        top = row(0) + top1
        zsum = jnp.zeros((1, tt), F32)
        for g in [g for g, _ in cands] + [hsum]:
            zsum = zsum + jnp.sum(jnp.where(g >= tau, jnp.exp(g - top), 0.0), axis=0, keepdims=True)
        inv_z = 1.0 / zsum
        cnt = lambda g: jnp.sum(jnp.where(g >= tau, 1.0, 0.0), axis=0, keepdims=True)
        k_rows = [cnt(cands[0][0]) + cnt(cands[1][0])] + [cnt(g) for g, _ in cands[2:]]
        k_hi = jnp.where(hsum >= tau, 1.0, 0.0)
        k1 = jnp.zeros(s1.shape, F32)
        for a in range(PEER_TOPK):
            ka = k_rows[a] if a < 8 else k_hi[a - 8:a - 7, :]
            k1 = jnp.where(s1 == row(a), ka, k1)
        e2 = jnp.exp(s2 - top1)
        c1 = jnp.exp(s1 - row(0)) * inv_z
        for lc in range(n_lc):
            cols = slice(lc * 128, (lc + 1) * 128)
            r2_s[h, lc] = rank2[:, cols].astype(BF16)
            e2_s[h, lc] = e2[:, cols].astype(BF16)
            k1_s[h, lc] = k1[:, cols]
            c1_s[h, lc] = c1[:, cols]

    def gate_rows(i, carry):
        for lc in range(n_lc):
            acc = None
            for h in range(PEER_HEADS):
                k1 = jnp.broadcast_to(k1_s[h, lc, pl.ds(i, 1), :], (16, 128)).astype(BF16)
                c1 = jnp.broadcast_to(c1_s[h, lc, pl.ds(i, 1), :], (16, 128)).astype(BF16)
                k1 = jnp.tile(k1, (PEER_NKEYS // 16, 1))
                c1 = jnp.tile(c1, (PEER_NKEYS // 16, 1))
                w = jnp.where(r2_s[h, lc] < k1, e2_s[h, lc] * c1, jnp.zeros_like(c1))
                acc = w if acc is None else acc + w
            w_ref[pl.ds(pl.multiple_of(i * PEER_NKEYS, PEER_NKEYS), PEER_NKEYS), lc * 128:(lc + 1) * 128] = acc
        return carry

    lax.fori_loop(0, PEER_NKEYS, gate_rows, 0, unroll=2)


def _peer_select(qp, keys16):
    t = qp.shape[0]
    tt = 256
    n_exp = PEER_NKEYS * PEER_NKEYS
    stage = pltpu.VMEM((PEER_HEADS, tt // 128, PEER_NKEYS, 128), F32)
    stage16 = pltpu.VMEM((PEER_HEADS, tt // 128, PEER_NKEYS, 128), BF16)
    return pl.pallas_call(
        _peer_select_kernel,
        grid=(t // tt,),
        in_specs=[
            pl.BlockSpec((tt, qp.shape[1]), lambda i: (i, 0)),
            pl.BlockSpec(keys16.shape, lambda i: (0, 0, 0)),
        ],
        out_specs=pl.BlockSpec((n_exp, tt), lambda i: (0, i)),
        out_shape=jax.ShapeDtypeStruct((n_exp, t), BF16),
        scratch_shapes=[stage16, stage16, stage, stage],
        compiler_params=_cparams(1, 48),
        name="peer_select",
    )(qp, keys16)


def _peer_dense_kernel(h_ref, u_ref, v_ref, w_ref, x_ref, mod_ref, o_ref):
    e = pl.program_id(1)

    @pl.when(e == 0)
    def _():
        o_ref[...] = jnp.zeros_like(o_ref)

    at = _dot_nt(u_ref[...], h_ref[...])
    gelu = 0.5 * at * (1.0 + lax.erf(at * float(np.sqrt(0.5))))
    wt = (w_ref[...].astype(F32) * gelu).astype(BF16)
    o_ref[...] += _dot_tn(wt, v_ref[...])

    @pl.when(e == pl.num_programs(1) - 1)
    def _():
        o_ref[...] = x_ref[...] + mod_ref[0, 5:6, :] * o_ref[...]


def _peer_dense(h2, u16, v16, wgt, x1, mod3, seq):
    t, d = h2.shape
    n_exp = u16.shape[0]
    tm = 512 if t % 512 == 0 else t
    tm = min(tm, seq)
    te = 512
    return pl.pallas_call(
        _peer_dense_kernel,
        grid=(t // tm, n_exp // te),
        in_specs=[
            pl.BlockSpec((tm, d), lambda i, e: (i, 0)),
            pl.BlockSpec((te, d), lambda i, e: (e, 0)),
            pl.BlockSpec((te, d), lambda i, e: (e, 0)),
            pl.BlockSpec((te, tm), lambda i, e: (e, i)),
            pl.BlockSpec((tm, d), lambda i, e: (i, 0)),
            pl.BlockSpec((1, N_MOD, d), lambda i, e: ((i * tm) // seq, 0, 0)),
        ],
        out_specs=pl.BlockSpec((tm, d), lambda i, e: (i, 0)),
        out_shape=jax.ShapeDtypeStruct((t, d), F32),
        compiler_params=_cparams(2, 60),
        name="peer_dense",
    )(h2, u16, v16, wgt, x1, mod3)


def _layer(x, c8, w_ada, b_ada, norm1_g, w_in, b_f, qn_m, kn_m, qn_f, kn_f, w_out, norm2_g,
           w_pq, sub_keys, peer_u, peer_v):
    b, s, d = x.shape
    t = b * s
    n_heads = d // HEAD_DIM
    n_moba = n_heads // 2
    n_fox = n_heads - n_moba
    d_moba = n_moba * HEAD_DIM
    d_fox = n_fox * HEAD_DIM
    d_qkv = 3 * d_moba + 3 * d_fox
    assert s % ATTN_TILE == 0 and ATTN_TILE % MOBA_BLOCK == 0 and s % 512 == 0
    assert n_fox <= FG_GROUP and s // MOBA_BLOCK <= 16 and b <= 8

    mod = _adaln(c8, w_ada, b_ada.reshape(1, -1))
    mod3 = mod[:b].reshape(b, N_MOD, d)

    w_in_t = jnp.swapaxes(w_in, 0, 1)
    w_fg_t = w_in_t[d_qkv:]
    wfg3 = jnp.zeros((128, d), F32)
    bfg3 = jnp.zeros((1, 128), F32)
    for g in range(3):
        wfg3 = wfg3.at[g * FG_GROUP:g * FG_GROUP + n_fox, :].set(w_fg_t)
        bfg3 = bfg3.at[0, g * FG_GROUP:g * FG_GROUP + n_fox].set(b_f)
    h1, fpack = _norm1(x, norm1_g.reshape(1, d), mod3, wfg3.astype(BF16), bfg3, n_fox)

    proj = _matmul(h1.reshape(t, d), w_in_t, d_qkv, F32, "in_proj", w_rows_are_outputs=True)

    slopes = 2.0 ** (-8.0 * jnp.arange(1, n_moba + 1, dtype=F32) / n_moba)
    o_moba, u16, wout16 = _moba(proj, slopes, qn_m.reshape(1, -1), kn_m.reshape(1, -1), b, s, n_moba, 0,
                                peer_u, w_out)
    o_fox, v16, wq16 = _fox(proj, fpack, qn_f.reshape(1, -1), kn_f.reshape(1, -1), b, s, n_fox, 3 * n_moba,
                            peer_v, w_pq)

    x1 = _outproj(o_moba, o_fox, wout16, x.reshape(t, d), mod3, s).reshape(b, s, d)

    h2, qp = _norm2_query(x1, norm2_g.reshape(1, d), mod3, wq16)
    h2 = h2.reshape(t, d)
    keys16 = sub_keys.reshape(2 * PEER_HEADS, PEER_NKEYS, PEER_HALF).astype(BF16)
    wgt = _peer_select(qp, keys16)
    out = _peer_dense(h2, u16, v16, wgt, x1.reshape(t, d), mod3, s)
    return out.reshape(b, s, d)


def kernel(x, c, w_ada, b_ada, norm1_g, w_in, b_f, q_norm_moba, k_norm_moba, q_norm_fox, k_norm_fox,
           w_out, norm2_g, w_pq, peer_sub_keys, peer_u, peer_v):
    b = x.shape[0]
    c8 = jnp.zeros((8, c.shape[1]), F32).at[:b].set(c)
    for l in range(w_ada.shape[0]):
        x = _layer(x, c8, w_ada[l], b_ada[l], norm1_g[l], w_in[l], b_f[l], q_norm_moba[l], k_norm_moba[l],
                   q_norm_fox[l], k_norm_fox[l], w_out[l], norm2_g[l], w_pq[l], peer_sub_keys[l],
                   peer_u[l], peer_v[l])
    return x
```
